```python
import jax, jax.numpy as jnp
from jax import lax
import numpy as np

D_MODEL = 2048
BATCH = 8
SEQ = 2048
DEPTH = 2

N_META = 16
MLSTM_W = D_MODEL // 2
CONV_W = D_MODEL - MLSTM_W
MLSTM_HEADS = 4
DV = MLSTM_W // MLSTM_HEADS
DQK = DV // 2
QK_W = MLSTM_HEADS * DQK
CHUNK = 64
CONV_K = 3
D_FF = -(-8 * D_MODEL // (3 * 256)) * 256
GATE_CAP = 15.0
EPS = 1e-6
SPLIT_SIZES = (QK_W, QK_W, MLSTM_W, MLSTM_W, MLSTM_HEADS, MLSTM_HEADS, CONV_W, CONV_W, CONV_W)
D_IN = sum(SPLIT_SIZES)

kernel_name = "hymba_mlstm_shortconv_swiglu"


def rmsnorm(x, w):
    xf = x.astype(jnp.float32)
    y = xf * lax.rsqrt(jnp.mean(xf * xf, axis=-1, keepdims=True) + EPS)
    return (y * w.astype(jnp.float32)).astype(x.dtype)


def mlstm_chunkwise(q, k, v, log_i, log_f):
    b_, h_, t_, _ = q.shape
    nc = t_ // CHUNK

    def to_chunks(a):
        return jnp.moveaxis(a.reshape(a.shape[:2] + (nc, CHUNK) + a.shape[3:]), 2, 0)

    causal = jnp.tril(jnp.ones((CHUNK, CHUNK), dtype=bool))

    def step(carry, inp):
        c_st, n_st, m_st = carry
        qb, kb, vb, li, lf = inp
        b = jnp.cumsum(lf, axis=-1)
        dmat = jnp.where(causal, b[..., :, None] - b[..., None, :] + li[..., None, :], -jnp.inf)
        inter = b + m_st[..., None]
        m_t = jnp.maximum(inter, jnp.max(dmat, axis=-1))
        w_inter = jnp.exp(inter - m_t)
        s_w = jnp.einsum('bhtd,bhsd->bhts', qb, kb) * jnp.exp(dmat - m_t[..., None])
        num = (w_inter[..., None] * jnp.einsum('bhtd,bhde->bhte', qb, c_st)
               + jnp.einsum('bhts,bhse->bhte', s_w, vb))
        den = w_inter * jnp.einsum('bhtd,bhd->bht', qb, n_st) + jnp.sum(s_w, axis=-1)
        h = num / jnp.maximum(jnp.abs(den), jnp.exp(-m_t))[..., None]
        b_end = b[..., -1]
        decay = b_end[..., None] - b + li
        m_new = jnp.maximum(b_end + m_st, jnp.max(decay, axis=-1))
        w_old = jnp.exp(b_end + m_st - m_new)
        w_in = jnp.exp(decay - m_new[..., None])
        c_new = w_old[..., None, None] * c_st + jnp.einsum('bhs,bhsd,bhse->bhde', w_in, kb, vb)
        n_new = w_old[..., None] * n_st + jnp.einsum('bhs,bhsd->bhd', w_in, kb)
        return (c_new, n_new, m_new), h

    init = (jnp.zeros((b_, h_, q.shape[-1], v.shape[-1]), jnp.float32),
            jnp.zeros((b_, h_, q.shape[-1]), jnp.float32),
            jnp.zeros((b_, h_), jnp.float32))
    _, hs = lax.scan(step, init, tuple(map(to_chunks, (q, k, v, log_i, log_f))))
    return jnp.moveaxis(hs, 0, 2).reshape(b_, h_, t_, v.shape[-1])


def mlstm_group(q, k, v, i_raw, f_raw):
    seq_len = q.shape[1]
    pad_front = (-N_META) % CHUNK
    pad_back = (-(pad_front + seq_len)) % CHUNK
    tr = lambda a: jnp.moveaxis(a.astype(jnp.float32), 1, 2)
    q, k, v = tr(q) * (DQK ** -0.5), tr(k), tr(v)
    log_i = tr(GATE_CAP * jnp.tanh(i_raw.astype(jnp.float32) / GATE_CAP))
    log_f = jax.nn.log_sigmoid(tr(GATE_CAP * jnp.tanh(f_raw.astype(jnp.float32) / GATE_CAP)))
    pad4 = ((0, 0), (0, 0), (pad_front, pad_back), (0, 0))
    pad3 = ((0, 0), (0, 0), (pad_front, pad_back))
    q, k, v = jnp.pad(q, pad4), jnp.pad(k, pad4), jnp.pad(v, pad4)
    log_i = jnp.pad(log_i, pad3, constant_values=-jnp.inf)
    log_f = jnp.pad(log_f, pad3)
    h = mlstm_chunkwise(q, k, v, log_i, log_f)
    return h[:, :, pad_front:pad_front + seq_len]


def short_conv_group(u, gate_b, gate_c, conv_w):
    a = gate_c * u
    seq_len = a.shape[1]
    ap = jnp.pad(a, ((0, 0), (CONV_K - 1, 0), (0, 0)))
    conv = sum(ap[:, j:j + seq_len] * conv_w[j] for j in range(CONV_K))
    return gate_b * conv


def setup_inputs(seed: int = 0) -> dict:
    key = jax.random.key(seed)
    ks = jax.random.split(key, 14)
    nrm = lambda k, shape, s: jax.random.normal(k, shape, jnp.float32) * s
    gain = lambda k, shape: 1.0 + 0.02 * jax.random.normal(k, shape, jnp.float32)
    b_i = nrm(ks[4], (DEPTH, MLSTM_HEADS), 0.1)
    b_f = 3.0 + nrm(ks[5], (DEPTH, MLSTM_HEADS), 0.5)
    return {
        "x": nrm(ks[0], (BATCH, SEQ, D_MODEL), 1.0),
        "meta_tokens": nrm(ks[1], (N_META, D_MODEL), 1.0),
        "norm_mix_w": gain(ks[2], (DEPTH, D_MODEL)),
        "w_in": nrm(ks[3], (DEPTH, D_MODEL, D_IN), D_MODEL ** -0.5),
        "b_gates": jnp.concatenate([b_i, b_f], axis=-1),
        "conv_w": nrm(ks[6], (DEPTH, CONV_K, CONV_W), CONV_K ** -0.5),
        "mlstm_norm_w": gain(ks[7], (DEPTH, MLSTM_W)),
        "w_out": nrm(ks[8], (DEPTH, D_MODEL, D_MODEL), D_MODEL ** -0.5),
        "norm_ffn_w": gain(ks[9], (DEPTH, D_MODEL)),
        "w_gate": nrm(ks[10], (DEPTH, D_MODEL, D_FF), D_MODEL ** -0.5),
        "w_up": nrm(ks[11], (DEPTH, D_MODEL, D_FF), D_MODEL ** -0.5),
        "w_down": nrm(ks[12], (DEPTH, D_FF, D_MODEL), D_FF ** -0.5),
        "norm_final_w": gain(ks[13], (D_MODEL,)),
    }


def reference(x, meta_tokens, norm_mix_w, w_in, b_gates, conv_w, mlstm_norm_w, w_out,
              norm_ffn_w, w_gate, w_up, w_down, norm_final_w):
    bsz = x.shape[0]
    meta = jnp.broadcast_to(meta_tokens.astype(x.dtype)[None], (bsz, N_META, D_MODEL))
    h = jnp.concatenate([meta, x], axis=1)
    seq_len = h.shape[1]
    split_points = np.cumsum(SPLIT_SIZES)[:-1].tolist()
    for l in range(DEPTH):
        hn = rmsnorm(h, norm_mix_w[l])
        proj = hn @ w_in[l]
        q, k, v, og, ig, fg, u, gb, gc = jnp.split(proj, split_points, axis=-1)
        ig = ig + b_gates[l, :MLSTM_HEADS]
        fg = fg + b_gates[l, MLSTM_HEADS:]
        hm = mlstm_group(q.reshape(bsz, seq_len, MLSTM_HEADS, DQK),
                         k.reshape(bsz, seq_len, MLSTM_HEADS, DQK),
                         v.reshape(bsz, seq_len, MLSTM_HEADS, DV), ig, fg)
        hm = rmsnorm(hm, mlstm_norm_w[l].reshape(MLSTM_HEADS, 1, DV))
        hm = jnp.moveaxis(hm, 1, 2).reshape(bsz, seq_len, MLSTM_W).astype(h.dtype)
        hm = jax.nn.sigmoid(og) * hm
        hc = short_conv_group(u, gb, gc, conv_w[l])
        h = h + jnp.concatenate([hm, hc], axis=-1) @ w_out[l]
        hf = rmsnorm(h, norm_ffn_w[l])
        h = h + (jax.nn.silu(hf @ w_gate[l]) * (hf @ w_up[l])) @ w_down[l]
    out = rmsnorm(h, norm_final_w)
    return out[:, N_META:]
```

```python
import functools

import jax
import jax.numpy as jnp
from jax import lax
from jax.experimental import pallas as pl
from jax.experimental.pallas import tpu as pltpu

D_MODEL = 2048
N_META = 16
MLSTM_W = D_MODEL // 2
CONV_W = D_MODEL - MLSTM_W
HEADS = 4
DV = MLSTM_W // HEADS
DQK = DV // 2
QK_W = HEADS * DQK
CONV_K = 3
GATE_CAP = 15.0
EPS = 1e-6
QK_SCALE = DQK ** -0.5

OFF_Q = 0
OFF_K = QK_W
OFF_V = 2 * QK_W
OFF_OG = OFF_V + MLSTM_W
OFF_U = OFF_OG + MLSTM_W
OFF_GB = OFF_U + CONV_W
OFF_GC = OFF_GB + CONV_W
PROJ_W = OFF_GC + CONV_W
GATE_COL0 = OFF_OG + MLSTM_W
N_GATES = 2 * HEADS

LANES = 128
SUBLANES = 8
CHUNK = 256
META_ROWS = CHUNK
VMEM_LIMIT = 56 * 1024 * 1024

F32 = jnp.float32
BF16 = jnp.bfloat16


def _rms_scale(x, w):
    ms = jnp.mean(x * x, axis=-1, keepdims=True)
    return x * lax.rsqrt(ms + EPS) * w


def _inproj_kernel(h_ref, nw_ref, w_ref, wg_ref, proj_ref, gates_ref, hn_ref):
    @pl.when(pl.program_id(1) == 0)
    def _():
        hn = _rms_scale(h_ref[...], nw_ref[...]).astype(BF16)
        hn_ref[...] = hn
        gates_ref[...] = jnp.dot(hn, wg_ref[...], preferred_element_type=F32)

    proj_ref[...] = jnp.dot(hn_ref[...], w_ref[...],
                            preferred_element_type=F32).astype(BF16)


def _inproj(h, nw, w, wg, *, tm, tn):
    m = h.shape[0]
    return pl.pallas_call(
        _inproj_kernel,
        grid=(m // tm, PROJ_W // tn),
        in_specs=[
            pl.BlockSpec((tm, D_MODEL), lambda i, j: (i, 0)),
            pl.BlockSpec((1, D_MODEL), lambda i, j: (0, 0)),
            pl.BlockSpec((D_MODEL, tn), lambda i, j: (0, j)),
            pl.BlockSpec((D_MODEL, LANES), lambda i, j: (0, 0)),
        ],
        out_specs=[
            pl.BlockSpec((tm, tn), lambda i, j: (i, j)),
            pl.BlockSpec((tm, LANES), lambda i, j: (i, 0)),
        ],
        out_shape=[
            jax.ShapeDtypeStruct((m, PROJ_W), BF16),
            jax.ShapeDtypeStruct((m, LANES), F32),
        ],
        scratch_shapes=[pltpu.VMEM((tm, D_MODEL), BF16)],
        compiler_params=pltpu.CompilerParams(
            dimension_semantics=("arbitrary", "arbitrary"),
            vmem_limit_bytes=VMEM_LIMIT),
        name="inproj",
    )(h, nw, w, wg)


def _log_sigmoid(x):
    return jnp.minimum(x, 0.0) - jnp.log1p(jnp.exp(-jnp.abs(x)))


def _cumsum_rows(x, tri):
    hi = x.astype(BF16)
    r1 = x - hi.astype(F32)
    mid = r1.astype(BF16)
    lo = (r1 - mid.astype(F32)).astype(BF16)
    dot = functools.partial(jnp.dot, preferred_element_type=F32)
    return dot(tri, hi) + dot(tri, mid) + dot(tri, lo)


def _mixer_kernel(proj_ref, gates_ref, bias_ref, convw_ref, mnw_ref,
                  c0_ref, n0_ref, m0_ref, t0_ref,
                  y_ref, cn_ref, nn_ref, mn_ref, tn_ref,
                  c_scr, n_scr, m_scr, a_scr, *, n_valid):
    c = CHUNK
    step = pl.program_id(1)

    @pl.when(step == 0)
    def _():
        c_scr[...] = c0_ref[...]
        n_scr[...] = n0_ref[...]
        m_scr[...] = m0_ref[...]
        a_scr[0:SUBLANES, :] = t0_ref[...]

    graw = gates_ref[...]
    g = graw + pltpu.roll(graw, LANES - N_GATES, axis=1) + bias_ref[...]
    capped = GATE_CAP * jnp.tanh(g / GATE_CAP)
    lane = lax.broadcasted_iota(jnp.int32, (c, LANES), 1)
    log_i = capped
    log_f = _log_sigmoid(capped)
    if n_valid < c:
        row_ok = lax.broadcasted_iota(jnp.int32, (c, LANES), 0) < n_valid
        log_i = jnp.where(row_ok, log_i, -jnp.inf)
        log_f = jnp.where(row_ok, log_f, 0.0)
    row = lax.broadcasted_iota(jnp.int32, (c, c), 0)
    col = lax.broadcasted_iota(jnp.int32, (c, c), 1)
    causal = row >= col
    tri = jnp.where(causal, 1.0, 0.0).astype(BF16)
    bcum = _cumsum_rows(jnp.where(lane >= HEADS, log_f, 0.0), tri)
    x_c = jnp.where(lane < HEADS, log_i, bcum)
    x_r = x_c.T

    for hd in range(HEADS):
        q = proj_ref[:, OFF_Q + hd * DQK:OFF_Q + (hd + 1) * DQK]
        k = proj_ref[:, OFF_K + hd * DQK:OFF_K + (hd + 1) * DQK]
        v = proj_ref[:, OFF_V + hd * DV:OFF_V + (hd + 1) * DV]
        og = proj_ref[:, OFF_OG + hd * DV:OFF_OG + (hd + 1) * DV].astype(F32)
        li_c = x_c[:, hd:hd + 1]
        b_c = x_c[:, HEADS + hd:HEADS + hd + 1]
        li_r = x_r[hd:hd + 1, :]
        b_r = x_r[HEADS + hd:HEADS + hd + 1, :]
        c_st = c_scr[hd]
        n_st = n_scr[hd]
        m_st = m_scr[hd][:, 0:1]

        dmat = jnp.where(causal, b_c - b_r + li_r, -jnp.inf)
        inter = b_c + m_st
        m_t = jnp.maximum(inter, jnp.max(dmat, axis=-1, keepdims=True))
        w_inter = jnp.exp(inter - m_t)
        s = lax.dot_general(q, k, (((1,), (1,)), ((), ())),
                            preferred_element_type=F32) * QK_SCALE
        s_w = s * jnp.exp(dmat - m_t)
        qc = jnp.dot(q, c_st.astype(BF16), preferred_element_type=F32) * QK_SCALE
        num = w_inter * qc + jnp.dot(s_w.astype(BF16), v, preferred_element_type=F32)
        qn = jnp.sum(q.astype(F32) * n_st, axis=-1, keepdims=True) * QK_SCALE
        den = w_inter * qn + jnp.sum(s_w, axis=-1, keepdims=True)
        hval = num / jnp.maximum(jnp.abs(den), jnp.exp(-m_t))

        b_end = b_c[c - 1:c, :]
        decay = b_end - b_c + li_c
        m_new = jnp.maximum(b_end + m_st, jnp.max(decay, axis=0, keepdims=True))
        w_old = jnp.exp(b_end + m_st - m_new)
        kw = k.astype(F32) * jnp.exp(decay - m_new)
        c_scr[hd] = w_old * c_st + jnp.dot(kw.T.astype(BF16), v,
                                           preferred_element_type=F32)
        n_scr[hd] = w_old * n_st + jnp.sum(kw, axis=0, keepdims=True)
        m_scr[hd] = jnp.broadcast_to(m_new, (1, LANES))

        hm = _rms_scale(hval, mnw_ref[:, hd * DV:(hd + 1) * DV])
        y_ref[:, hd * DV:(hd + 1) * DV] = (jax.nn.sigmoid(og) * hm).astype(BF16)

    u = proj_ref[:, OFF_U:OFF_U + CONV_W].astype(F32)
    gb = proj_ref[:, OFF_GB:OFF_GB + CONV_W].astype(F32)
    gc = proj_ref[:, OFF_GC:OFF_GC + CONV_W].astype(F32)
    a = gc * u
    a_scr[SUBLANES:SUBLANES + c, :] = a
    a1 = a_scr[SUBLANES - 1:SUBLANES - 1 + c, :]
    a2 = a_scr[SUBLANES - 2:SUBLANES - 2 + c, :]
    conv = a2 * convw_ref[0:1, :] + a1 * convw_ref[1:2, :] + a * convw_ref[2:3, :]
    y_ref[:, MLSTM_W:MLSTM_W + CONV_W] = (gb * conv).astype(BF16)
    a_scr[0:SUBLANES, :] = a_scr[n_valid:n_valid + SUBLANES, :]

    @pl.when(step == pl.num_programs(1) - 1)
    def _():
        cn_ref[...] = c_scr[...]
        nn_ref[...] = n_scr[...]
        mn_ref[...] = m_scr[...]
        tn_ref[...] = a_scr[0:SUBLANES, :]


def _mixer(proj, gates, bias, convw, mnw, state, *, n_seq, n_valid):
    m = proj.shape[0]
    n_chunks = m // (n_seq * CHUNK)
    rows = lambda b, j: (b * n_chunks + j, 0)
    const2 = lambda b, j: (0, 0)
    const3 = lambda b, j: (0, 0, 0)
    state_specs = [
        pl.BlockSpec((HEADS, DQK, DV), const3),
        pl.BlockSpec((HEADS, 1, LANES), const3),
        pl.BlockSpec((HEADS, 1, LANES), const3),
        pl.BlockSpec((SUBLANES, CONV_W), const2),
    ]
    state_shapes = [
        jax.ShapeDtypeStruct((HEADS, DQK, DV), F32),
        jax.ShapeDtypeStruct((HEADS, 1, LANES), F32),
        jax.ShapeDtypeStruct((HEADS, 1, LANES), F32),
        jax.ShapeDtypeStruct((SUBLANES, CONV_W), F32),
    ]
    out = pl.pallas_call(
        functools.partial(_mixer_kernel, n_valid=n_valid),
        grid=(n_seq, n_chunks),
        in_specs=[
            pl.BlockSpec((CHUNK, PROJ_W), rows),
            pl.BlockSpec((CHUNK, LANES), rows),
            pl.BlockSpec((1, LANES), const2),
            pl.BlockSpec((SUBLANES, CONV_W), const2),
            pl.BlockSpec((1, MLSTM_W), const2),
        ] + state_specs,
        out_specs=[pl.BlockSpec((CHUNK, D_MODEL), rows)] + state_specs,
        out_shape=[jax.ShapeDtypeStruct((m, D_MODEL), BF16)] + state_shapes,
        scratch_shapes=[
            pltpu.VMEM((HEADS, DQK, DV), F32),
            pltpu.VMEM((HEADS, 1, LANES), F32),
            pltpu.VMEM((HEADS, 1, LANES), F32),
            pltpu.VMEM((SUBLANES + CHUNK, CONV_W), F32),
        ],
        compiler_params=pltpu.CompilerParams(
            dimension_semantics=("arbitrary", "arbitrary"),
            vmem_limit_bytes=VMEM_LIMIT),
        name="mixer",
    )(proj, gates, bias, convw, mnw, *state)
    return out[0], tuple(out[1:])


def _outproj_kernel(y_ref, w_ref, h_ref, nw_ref, hnew_ref, hf_ref):
    hnew = h_ref[...] + jnp.dot(y_ref[...], w_ref[...], preferred_element_type=F32)
    hnew_ref[...] = hnew
    hf_ref[...] = _rms_scale(hnew, nw_ref[...]).astype(BF16)


def _outproj(y, w, h, nw, *, tm):
    m = y.shape[0]
    rows = lambda i: (i, 0)
    const = lambda i: (0, 0)
    return pl.pallas_call(
        _outproj_kernel,
        grid=(m // tm,),
        in_specs=[
            pl.BlockSpec((tm, D_MODEL), rows),
            pl.BlockSpec((D_MODEL, D_MODEL), const),
            pl.BlockSpec((tm, D_MODEL), rows),
            pl.BlockSpec((1, D_MODEL), const),
        ],
        out_specs=[pl.BlockSpec((tm, D_MODEL), rows),
                   pl.BlockSpec((tm, D_MODEL), rows)],
        out_shape=[jax.ShapeDtypeStruct((m, D_MODEL), F32),
                   jax.ShapeDtypeStruct((m, D_MODEL), BF16)],
        compiler_params=pltpu.CompilerParams(
            dimension_semantics=("arbitrary",),
            vmem_limit_bytes=VMEM_LIMIT),
        name="outproj",
    )(y, w, h, nw)


def _ffn_kernel(hf_ref, h_ref, wg_ref, wu_ref, wd_ref, out_ref):
    @pl.when(pl.program_id(1) == 0)
    def _():
        out_ref[...] = h_ref[...]

    x = hf_ref[...]
    g = jnp.dot(x, wg_ref[...], preferred_element_type=F32)
    u = jnp.dot(x, wu_ref[...], preferred_element_type=F32)
    act = (g * jax.nn.sigmoid(g) * u).astype(BF16)
    out_ref[...] += jnp.dot(act, wd_ref[...], preferred_element_type=F32)


def _ffn(hf, h, wg, wu, wd, *, tm, tf):
    m = hf.shape[0]
    d_ff = wg.shape[1]
    rows = lambda i, k: (i, 0)
    return pl.pallas_call(
        _ffn_kernel,
        grid=(m // tm, d_ff // tf),
        in_specs=[
            pl.BlockSpec((tm, D_MODEL), rows),
            pl.BlockSpec((tm, D_MODEL), rows),
            pl.BlockSpec((D_MODEL, tf), lambda i, k: (0, k)),
            pl.BlockSpec((D_MODEL, tf), lambda i, k: (0, k)),
            pl.BlockSpec((tf, D_MODEL), lambda i, k: (k, 0)),
        ],
        out_specs=pl.BlockSpec((tm, D_MODEL), rows),
        out_shape=jax.ShapeDtypeStruct((m, D_MODEL), F32),
        compiler_params=pltpu.CompilerParams(
            dimension_semantics=("arbitrary", "arbitrary"),
            vmem_limit_bytes=VMEM_LIMIT),
        name="ffn",
    )(hf, h, wg, wu, wd)


def _final_norm_kernel(h_ref, nw_ref, out_ref):
    out_ref[...] = _rms_scale(h_ref[...], nw_ref[...])


def _final_norm(h, nw, *, tm):
    m = h.shape[0]
    return pl.pallas_call(
        _final_norm_kernel,
        grid=(m // tm,),
        in_specs=[pl.BlockSpec((tm, D_MODEL), lambda i: (i, 0)),
                  pl.BlockSpec((1, D_MODEL), lambda i: (0, 0))],
        out_specs=pl.BlockSpec((tm, D_MODEL), lambda i: (i, 0)),
        out_shape=jax.ShapeDtypeStruct((m, D_MODEL), F32),
        compiler_params=pltpu.CompilerParams(
            dimension_semantics=("arbitrary",),
            vmem_limit_bytes=VMEM_LIMIT),
        name="final_norm",
    )(h, nw)


def _pack_in_weights(w_in, b_gates):
    w_main = jnp.concatenate(
        [w_in[:, :GATE_COL0], w_in[:, GATE_COL0 + N_GATES:]], axis=1).astype(BF16)
    wg = w_in[:, GATE_COL0:GATE_COL0 + N_GATES]
    wg_hi = wg.astype(BF16)
    wg_lo = (wg - wg_hi.astype(F32)).astype(BF16)
    wg_packed = jnp.concatenate(
        [wg_hi, wg_lo, jnp.zeros((D_MODEL, LANES - 2 * N_GATES), BF16)], axis=1)
    bias = jnp.concatenate([b_gates, jnp.zeros((LANES - N_GATES,), F32)])[None, :]
    return w_main, wg_packed, bias


def kernel(x, meta_tokens, norm_mix_w, w_in, b_gates, conv_w, mlstm_norm_w, w_out,
           norm_ffn_w, w_gate, w_up, w_down, norm_final_w):
    bsz, seq, _ = x.shape
    depth = w_in.shape[0]
    assert seq % CHUNK == 0 and N_META <= META_ROWS

    h_x = x.reshape(bsz * seq, D_MODEL)
    h_m = jnp.concatenate(
        [meta_tokens.astype(x.dtype), jnp.zeros((META_ROWS - N_META, D_MODEL), x.dtype)])

    zero_state = (jnp.zeros((HEADS, DQK, DV), F32),
                  jnp.zeros((HEADS, 1, LANES), F32),
                  jnp.zeros((HEADS, 1, LANES), F32),
                  jnp.zeros((SUBLANES, CONV_W), F32))

    for l in range(depth):
        w_main, wg_packed, bias = _pack_in_weights(w_in[l], b_gates[l])
        nw_mix = norm_mix_w[l][None, :]
        nw_ffn = norm_ffn_w[l][None, :]
        mnw = mlstm_norm_w[l][None, :]
        convw = jnp.concatenate(
            [conv_w[l], jnp.zeros((SUBLANES - CONV_K, CONV_W), F32)])
        wo = w_out[l].astype(BF16)
        wg = w_gate[l].astype(BF16)
        wu = w_up[l].astype(BF16)
        wd = w_down[l].astype(BF16)

        proj, gates = _inproj(h_m, nw_mix, w_main, wg_packed, tm=META_ROWS, tn=512)
        y, state = _mixer(proj, gates, bias, convw, mnw, zero_state,
                          n_seq=1, n_valid=N_META)
        h_m, hf = _outproj(y, wo, h_m, nw_ffn, tm=META_ROWS)
        h_m = _ffn(hf, h_m, wg, wu, wd, tm=META_ROWS, tf=512)

        proj, gates = _inproj(h_x, nw_mix, w_main, wg_packed, tm=1024, tn=512)
        y, _ = _mixer(proj, gates, bias, convw, mnw, state, n_seq=bsz, n_valid=CHUNK)
        h_x, hf = _outproj(y, wo, h_x, nw_ffn, tm=512)
        h_x = _ffn(hf, h_x, wg, wu, wd, tm=512, tf=512)

    out = _final_norm(h_x, norm_final_w[None, :], tm=512)
    return out.reshape(bsz, seq, D_MODEL)
```

```python
import functools

import jax
import jax.numpy as jnp
from jax import lax
from jax.experimental import pallas as pl
from jax.experimental.pallas import tpu as pltpu

D_MODEL = 2048
N_META = 16
MLSTM_W = D_MODEL // 2
CONV_W = D_MODEL - MLSTM_W
HEADS = 4
DV = MLSTM_W // HEADS
DQK = DV // 2
QK_W = HEADS * DQK
CONV_K = 3
GATE_CAP = 15.0
EPS = 1e-6
QK_SCALE = DQK ** -0.5

OFF_Q = 0
OFF_K = QK_W
OFF_V = 2 * QK_W
OFF_OG = OFF_V + MLSTM_W
OFF_U = OFF_OG + MLSTM_W
OFF_GB = OFF_U + CONV_W
OFF_GC = OFF_GB + CONV_W
PROJ_W = OFF_GC + CONV_W
GATE_COL0 = OFF_OG + MLSTM_W
N_GATES = 2 * HEADS
D_IN = PROJ_W + N_GATES

LANES = 128
SUBLANES = 8
CHUNK = 256
META_ROWS = CHUNK
VMEM_LIMIT = 56 * 1024 * 1024

F32 = jnp.float32
BF16 = jnp.bfloat16


def _params(*semantics):
    return pltpu.CompilerParams(dimension_semantics=semantics,
                                vmem_limit_bytes=VMEM_LIMIT)


def _rms_scale(x, w):
    ms = jnp.mean(x * x, axis=-1, keepdims=True)
    return x * lax.rsqrt(ms + EPS) * w


def _cast_kernel(x_ref, o_ref):
    o_ref[...] = x_ref[...].astype(BF16)


def _cast_bf16(w, *, rb):
    depth, rows, cols = w.shape
    spec = pl.BlockSpec((None, rb, cols), lambda l, i: (l, i, 0))
    return pl.pallas_call(
        _cast_kernel,
        grid=(depth, rows // rb),
        in_specs=[spec],
        out_specs=spec,
        out_shape=jax.ShapeDtypeStruct(w.shape, BF16),
        compiler_params=_params("arbitrary", "arbitrary"),
        name="cast_bf16",
    )(w)


def _pack_win_kernel(x_ref, main_ref, gate_ref):
    main_ref[:, 0:GATE_COL0] = x_ref[:, 0:GATE_COL0].astype(BF16)
    main_ref[:, GATE_COL0:PROJ_W] = x_ref[:, GATE_COL0 + N_GATES:D_IN].astype(BF16)
    w = x_ref[:, GATE_COL0:GATE_COL0 + LANES]
    hi = w.astype(BF16).astype(F32)
    lo = pltpu.roll(w - hi, N_GATES, axis=1)
    lane = lax.broadcasted_iota(jnp.int32, w.shape, 1)
    gate_ref[...] = jnp.where(lane < N_GATES, hi,
                              jnp.where(lane < 2 * N_GATES, lo, 0.0)).astype(BF16)


def _pack_win(w_in, *, rb):
    depth = w_in.shape[0]
    return pl.pallas_call(
        _pack_win_kernel,
        grid=(depth, D_MODEL // rb),
        in_specs=[pl.BlockSpec((None, rb, D_IN), lambda l, i: (l, i, 0))],
        out_specs=[pl.BlockSpec((None, rb, PROJ_W), lambda l, i: (l, i, 0)),
                   pl.BlockSpec((None, rb, LANES), lambda l, i: (l, i, 0))],
        out_shape=[jax.ShapeDtypeStruct((depth, D_MODEL, PROJ_W), BF16),
                   jax.ShapeDtypeStruct((depth, D_MODEL, LANES), BF16)],
        compiler_params=_params("arbitrary", "arbitrary"),
        name="pack_win",
    )(w_in)


def _inproj_kernel(h_ref, nw_ref, w_ref, wg_ref, proj_ref, gates_ref, hn_ref):
    @pl.when(pl.program_id(1) == 0)
    def _():
        hn = _rms_scale(h_ref[...], nw_ref[...]).astype(BF16)
        hn_ref[...] = hn
        gates_ref[...] = jnp.dot(hn, wg_ref[...], preferred_element_type=F32)

    proj_ref[...] = jnp.dot(hn_ref[...], w_ref[...],
                            preferred_element_type=F32).astype(BF16)


def _inproj(h, nw, w, wg, layer, *, tm, tn):
    m = h.shape[0]
    return pl.pallas_call(
        _inproj_kernel,
        grid=(m // tm, PROJ_W // tn),
        in_specs=[
            pl.BlockSpec((tm, D_MODEL), lambda i, j: (i, 0)),
            pl.BlockSpec((None, 1, D_MODEL), lambda i, j: (layer, 0, 0)),
            pl.BlockSpec((None, D_MODEL, tn), lambda i, j: (layer, 0, j)),
            pl.BlockSpec((None, D_MODEL, LANES), lambda i, j: (layer, 0, 0)),
        ],
        out_specs=[
            pl.BlockSpec((tm, tn), lambda i, j: (i, j)),
            pl.BlockSpec((tm, LANES), lambda i, j: (i, 0)),
        ],
        out_shape=[
            jax.ShapeDtypeStruct((m, PROJ_W), BF16),
            jax.ShapeDtypeStruct((m, LANES), F32),
        ],
        scratch_shapes=[pltpu.VMEM((tm, D_MODEL), BF16)],
        compiler_params=_params("arbitrary", "arbitrary"),
        name="inproj",
    )(h, nw, w, wg)


def _log_sigmoid(x):
    return jnp.minimum(x, 0.0) - jnp.log1p(jnp.exp(-jnp.abs(x)))


def _cumsum_rows(x, tri):
    hi = x.astype(BF16)
    r1 = x - hi.astype(F32)
    mid = r1.astype(BF16)
    lo = (r1 - mid.astype(F32)).astype(BF16)
    dot = functools.partial(jnp.dot, preferred_element_type=F32)
    return dot(tri, hi) + dot(tri, mid) + dot(tri, lo)


def _mixer_kernel(proj_ref, gates_ref, bias_ref, convw_ref, mnw_ref,
                  c0_ref, n0_ref, m0_ref, t0_ref,
                  y_ref, cn_ref, nn_ref, mn_ref, tn_ref,
                  c_scr, n_scr, m_scr, a_scr, *, n_valid):
    c = CHUNK
    step = pl.program_id(1)

    @pl.when(step == 0)
    def _():
        c_scr[...] = c0_ref[...]
        n_scr[...] = n0_ref[...]
        m_scr[...] = m0_ref[...]
        a_scr[0:SUBLANES, :] = t0_ref[...]

    graw = gates_ref[...]
    g = graw + pltpu.roll(graw, LANES - N_GATES, axis=1) + bias_ref[...]
    capped = GATE_CAP * jnp.tanh(g / GATE_CAP)
    lane = lax.broadcasted_iota(jnp.int32, (c, LANES), 1)
    log_i = capped
    log_f = _log_sigmoid(capped)
    if n_valid < c:
        row_ok = lax.broadcasted_iota(jnp.int32, (c, LANES), 0) < n_valid
        log_i = jnp.where(row_ok, log_i, -jnp.inf)
        log_f = jnp.where(row_ok, log_f, 0.0)
    row = lax.broadcasted_iota(jnp.int32, (c, c), 0)
    col = lax.broadcasted_iota(jnp.int32, (c, c), 1)
    causal = row >= col
    tri = jnp.where(causal, 1.0, 0.0).astype(BF16)
    bcum = _cumsum_rows(jnp.where(lane >= HEADS, log_f, 0.0), tri)
    x_c = jnp.where(lane < HEADS, log_i, bcum)
    x_r = x_c.T

    for hd in range(HEADS):
        q = proj_ref[:, OFF_Q + hd * DQK:OFF_Q + (hd + 1) * DQK]
        k = proj_ref[:, OFF_K + hd * DQK:OFF_K + (hd + 1) * DQK]
        v = proj_ref[:, OFF_V + hd * DV:OFF_V + (hd + 1) * DV]
        og = proj_ref[:, OFF_OG + hd * DV:OFF_OG + (hd + 1) * DV].astype(F32)
        li_c = x_c[:, hd:hd + 1]
        b_c = x_c[:, HEADS + hd:HEADS + hd + 1]
        li_r = x_r[hd:hd + 1, :]
        b_r = x_r[HEADS + hd:HEADS + hd + 1, :]
        c_st = c_scr[hd]
        n_st = n_scr[hd]
        m_st = m_scr[hd][:, 0:1]

        dmat = jnp.where(causal, b_c - b_r + li_r, -jnp.inf)
        inter = b_c + m_st
        m_t = jnp.maximum(inter, jnp.max(dmat, axis=-1, keepdims=True))
        w_inter = jnp.exp(inter - m_t)
        s = lax.dot_general(q, k, (((1,), (1,)), ((), ())),
                            preferred_element_type=F32) * QK_SCALE
        s_w = s * jnp.exp(dmat - m_t)
        qc = jnp.dot(q, c_st.astype(BF16), preferred_element_type=F32) * QK_SCALE
        num = w_inter * qc + jnp.dot(s_w.astype(BF16), v, preferred_element_type=F32)
        qn = jnp.sum(q.astype(F32) * n_st, axis=-1, keepdims=True) * QK_SCALE
        den = w_inter * qn + jnp.sum(s_w, axis=-1, keepdims=True)
        hval = num / jnp.maximum(jnp.abs(den), jnp.exp(-m_t))

        b_end = b_c[c - 1:c, :]
        decay = b_end - b_c + li_c
        m_new = jnp.maximum(b_end + m_st, jnp.max(decay, axis=0, keepdims=True))
        w_old = jnp.exp(b_end + m_st - m_new)
        kw = k.astype(F32) * jnp.exp(decay - m_new)
        c_scr[hd] = w_old * c_st + jnp.dot(kw.T.astype(BF16), v,
                                           preferred_element_type=F32)
        n_scr[hd] = w_old * n_st + jnp.sum(kw, axis=0, keepdims=True)
        m_scr[hd] = jnp.broadcast_to(m_new, (1, LANES))

        hm = _rms_scale(hval, mnw_ref[:, hd * DV:(hd + 1) * DV])
        y_ref[:, hd * DV:(hd + 1) * DV] = (jax.nn.sigmoid(og) * hm).astype(BF16)

    u = proj_ref[:, OFF_U:OFF_U + CONV_W].astype(F32)
    gb = proj_ref[:, OFF_GB:OFF_GB + CONV_W].astype(F32)
    gc = proj_ref[:, OFF_GC:OFF_GC + CONV_W].astype(F32)
    a = gc * u
    a_scr[SUBLANES:SUBLANES + c, :] = a
    a1 = a_scr[SUBLANES - 1:SUBLANES - 1 + c, :]
    a2 = a_scr[SUBLANES - 2:SUBLANES - 2 + c, :]
    conv = a2 * convw_ref[0:1, :] + a1 * convw_ref[1:2, :] + a * convw_ref[2:3, :]
    y_ref[:, MLSTM_W:MLSTM_W + CONV_W] = (gb * conv).astype(BF16)
    a_scr[0:SUBLANES, :] = a_scr[n_valid:n_valid + SUBLANES, :]

    @pl.when(step == pl.num_programs(1) - 1)
    def _():
        cn_ref[...] = c_scr[...]
        nn_ref[...] = n_scr[...]
        mn_ref[...] = m_scr[...]
        tn_ref[...] = a_scr[0:SUBLANES, :]


def _mixer(proj, gates, bias, convw, mnw, state, layer, *, n_seq, n_valid):
    m = proj.shape[0]
    n_chunks = m // (n_seq * CHUNK)
    rows = lambda b, j: (b * n_chunks + j, 0)
    const2 = lambda b, j: (0, 0)
    const3 = lambda b, j: (0, 0, 0)
    per_layer = lambda b, j: (layer, 0, 0)
    state_specs = [
        pl.BlockSpec((HEADS, DQK, DV), const3),
        pl.BlockSpec((HEADS, 1, LANES), const3),
        pl.BlockSpec((HEADS, 1, LANES), const3),
        pl.BlockSpec((SUBLANES, CONV_W), const2),
    ]
    state_shapes = [
        jax.ShapeDtypeStruct((HEADS, DQK, DV), F32),
        jax.ShapeDtypeStruct((HEADS, 1, LANES), F32),
        jax.ShapeDtypeStruct((HEADS, 1, LANES), F32),
        jax.ShapeDtypeStruct((SUBLANES, CONV_W), F32),
    ]
    out = pl.pallas_call(
        functools.partial(_mixer_kernel, n_valid=n_valid),
        grid=(n_seq, n_chunks),
        in_specs=[
            pl.BlockSpec((CHUNK, PROJ_W), rows),
            pl.BlockSpec((CHUNK, LANES), rows),
            pl.BlockSpec((None, 1, LANES), per_layer),
            pl.BlockSpec((None, SUBLANES, CONV_W), per_layer),
            pl.BlockSpec((None, 1, MLSTM_W), per_layer),
        ] + state_specs,
        out_specs=[pl.BlockSpec((CHUNK, D_MODEL), rows)] + state_specs,
        out_shape=[jax.ShapeDtypeStruct((m, D_MODEL), BF16)] + state_shapes,
        scratch_shapes=[
            pltpu.VMEM((HEADS, DQK, DV), F32),
            pltpu.VMEM((HEADS, 1, LANES), F32),
            pltpu.VMEM((HEADS, 1, LANES), F32),
            pltpu.VMEM((SUBLANES + CHUNK, CONV_W), F32),
        ],
        compiler_params=_params("arbitrary", "arbitrary"),
        name="mixer",
    )(proj, gates, bias, convw, mnw, *state)
    return out[0], tuple(out[1:])


def _outproj_kernel(y_ref, w_ref, h_ref, nw_ref, hnew_ref, hf_ref):
    hnew = h_ref[...] + jnp.dot(y_ref[...], w_ref[...], preferred_element_type=F32)
    hnew_ref[...] = hnew
    hf_ref[...] = _rms_scale(hnew, nw_ref[...]).astype(BF16)


def _outproj(y, w, h, nw, layer, *, tm):
    m = y.shape[0]
    rows = lambda i: (i, 0)
    return pl.pallas_call(
        _outproj_kernel,
        grid=(m // tm,),
        in_specs=[
            pl.BlockSpec((tm, D_MODEL), rows),
            pl.BlockSpec((None, D_MODEL, D_MODEL), lambda i: (layer, 0, 0)),
            pl.BlockSpec((tm, D_MODEL), rows),
            pl.BlockSpec((None, 1, D_MODEL), lambda i: (layer, 0, 0)),
        ],
        out_specs=[pl.BlockSpec((tm, D_MODEL), rows),
                   pl.BlockSpec((tm, D_MODEL), rows)],
        out_shape=[jax.ShapeDtypeStruct((m, D_MODEL), F32),
                   jax.ShapeDtypeStruct((m, D_MODEL), BF16)],
        compiler_params=_params("arbitrary"),
        name="outproj",
    )(y, w, h, nw)


def _ffn_kernel(hf_ref, h_hbm, wg_ref, wu_ref, wd_ref, nw_ref, out_ref, hbuf, sem,
                *, tm, norm_out):
    i = pl.program_id(0)
    k = pl.program_id(1)

    def residual_copy(tile):
        return pltpu.make_async_copy(h_hbm.at[pl.ds(tile * tm, tm), :], hbuf, sem)

    @pl.when(k == 0)
    def _():
        @pl.when(i == 0)
        def _():
            residual_copy(0).start()

        residual_copy(i).wait()
        out_ref[...] = hbuf[...]

        @pl.when(i + 1 < pl.num_programs(0))
        def _():
            residual_copy(i + 1).start()

    x = hf_ref[...]
    g = jnp.dot(x, wg_ref[...], preferred_element_type=F32)
    u = jnp.dot(x, wu_ref[...], preferred_element_type=F32)
    act = (g * jax.nn.sigmoid(g) * u).astype(BF16)
    out_ref[...] += jnp.dot(act, wd_ref[...], preferred_element_type=F32)

    if norm_out:
        @pl.when(k == pl.num_programs(1) - 1)
        def _():
            out_ref[...] = _rms_scale(out_ref[...], nw_ref[...])


def _ffn(hf, h, wg, wu, wd, layer, nw_out, *, tm, tf, norm_out):
    m = hf.shape[0]
    d_ff = wg.shape[2]
    rows = lambda i, k: (i, 0)
    return pl.pallas_call(
        functools.partial(_ffn_kernel, tm=tm, norm_out=norm_out),
        grid=(m // tm, d_ff // tf),
        in_specs=[
            pl.BlockSpec((tm, D_MODEL), rows),
            pl.BlockSpec(memory_space=pl.ANY),
            pl.BlockSpec((None, D_MODEL, tf), lambda i, k: (layer, 0, k)),
            pl.BlockSpec((None, D_MODEL, tf), lambda i, k: (layer, 0, k)),
            pl.BlockSpec((None, tf, D_MODEL), lambda i, k: (layer, k, 0)),
            pl.BlockSpec((1, D_MODEL), lambda i, k: (0, 0)),
        ],
        out_specs=pl.BlockSpec((tm, D_MODEL), rows),
        out_shape=jax.ShapeDtypeStruct((m, D_MODEL), F32),
        scratch_shapes=[pltpu.VMEM((tm, D_MODEL), F32),
                        pltpu.SemaphoreType.DMA(())],
        compiler_params=_params("arbitrary", "arbitrary"),
        name="ffn",
    )(hf, h, wg, wu, wd, nw_out)


def kernel(x, meta_tokens, norm_mix_w, w_in, b_gates, conv_w, mlstm_norm_w, w_out,
           norm_ffn_w, w_gate, w_up, w_down, norm_final_w):
    bsz, seq, _ = x.shape
    depth = w_in.shape[0]
    assert seq % CHUNK == 0 and N_META <= META_ROWS

    h_x = x.reshape(bsz * seq, D_MODEL)
    h_m = jnp.concatenate(
        [meta_tokens.astype(x.dtype), jnp.zeros((META_ROWS - N_META, D_MODEL), x.dtype)])

    w_main, w_gates = _pack_win(w_in, rb=256)
    wo = _cast_bf16(w_out, rb=512)
    wg = _cast_bf16(w_gate, rb=256)
    wu = _cast_bf16(w_up, rb=256)
    wd = _cast_bf16(w_down, rb=512)
    bias = jnp.pad(b_gates, ((0, 0), (0, LANES - N_GATES)))[:, None, :]
    convw = jnp.pad(conv_w, ((0, 0), (0, SUBLANES - CONV_K), (0, 0)))
    nw_mix = norm_mix_w[:, None, :]
    nw_ffn = norm_ffn_w[:, None, :]
    mnw = mlstm_norm_w[:, None, :]
    nw_final = norm_final_w[None, :]

    zero_state = (jnp.zeros((HEADS, DQK, DV), F32),
                  jnp.zeros((HEADS, 1, LANES), F32),
                  jnp.zeros((HEADS, 1, LANES), F32),
                  jnp.zeros((SUBLANES, CONV_W), F32))

    for l in range(depth):
        last = l == depth - 1
        proj, gates = _inproj(h_m, nw_mix, w_main, w_gates, l, tm=META_ROWS, tn=1024)
        y, state = _mixer(proj, gates, bias, convw, mnw, zero_state, l,
                          n_seq=1, n_valid=N_META)
        h_m, hf = _outproj(y, wo, h_m, nw_ffn, l, tm=META_ROWS)
        h_m = _ffn(hf, h_m, wg, wu, wd, l, nw_final, tm=META_ROWS, tf=512,
                   norm_out=False)

        proj, gates = _inproj(h_x, nw_mix, w_main, w_gates, l, tm=1024, tn=1024)
        y, _ = _mixer(proj, gates, bias, convw, mnw, state, l,
                      n_seq=bsz, n_valid=CHUNK)
        h_x, hf = _outproj(y, wo, h_x, nw_ffn, l, tm=512)
        h_x = _ffn(hf, h_x, wg, wu, wd, l, nw_final, tm=1024, tf=512, norm_out=last)

    return h_x.reshape(bsz, seq, D_MODEL)
```

```python
import functools

import jax
import jax.numpy as jnp
from jax import lax
from jax.experimental import pallas as pl
from jax.experimental.pallas import tpu as pltpu

D_MODEL = 2048
N_META = 16
MLSTM_W = D_MODEL // 2
CONV_W = D_MODEL - MLSTM_W
HEADS = 4
DV = MLSTM_W // HEADS
DQK = DV // 2
QK_W = HEADS * DQK
CONV_K = 3
GATE_CAP = 15.0
EPS = 1e-6
QK_SCALE = DQK ** -0.5

OFF_Q = 0
OFF_K = QK_W
OFF_V = 2 * QK_W
OFF_OG = OFF_V + MLSTM_W
OFF_U = OFF_OG + MLSTM_W
OFF_GB = OFF_U + CONV_W
OFF_GC = OFF_GB + CONV_W
PROJ_W = OFF_GC + CONV_W
GATE_COL0 = OFF_OG + MLSTM_W
N_GATES = 2 * HEADS
D_IN = PROJ_W + N_GATES

LANES = 128
SUBLANES = 8
CHUNK = 256
META_ROWS = CHUNK
VMEM_LIMIT = 56 * 1024 * 1024

F32 = jnp.float32
BF16 = jnp.bfloat16


def _params(*semantics):
    return pltpu.CompilerParams(dimension_semantics=semantics,
                                vmem_limit_bytes=VMEM_LIMIT)


def _rms_scale(x, w):
    ms = jnp.mean(x * x, axis=-1, keepdims=True)
    return x * lax.rsqrt(ms + EPS) * w


def _cast_kernel(x_ref, o_ref):
    o_ref[...] = x_ref[...].astype(BF16)


def _cast_bf16(w, *, rb):
    depth, rows, cols = w.shape
    spec = pl.BlockSpec((None, rb, cols), lambda l, i: (l, i, 0))
    return pl.pallas_call(
        _cast_kernel,
        grid=(depth, rows // rb),
        in_specs=[spec],
        out_specs=spec,
        out_shape=jax.ShapeDtypeStruct(w.shape, BF16),
        compiler_params=_params("arbitrary", "arbitrary"),
        name="cast_bf16",
    )(w)


def _pack_win_kernel(a_ref, b_ref, main_ref, gate_ref, *, tb):
    j = pl.program_id(1)
    first_shifted = GATE_COL0 // tb

    @pl.when(j < first_shifted)
    def _():
        main_ref[...] = a_ref[...].T.astype(BF16)

    @pl.when(j >= first_shifted)
    def _():
        x = jnp.concatenate([a_ref[N_GATES:, :], b_ref[...]], axis=0)
        main_ref[...] = x.T.astype(BF16)

    @pl.when(j == first_shifted)
    def _():
        w = a_ref[0:LANES, :].T
        hi = w.astype(BF16).astype(F32)
        lo = pltpu.roll(w - hi, N_GATES, axis=1)
        lane = lax.broadcasted_iota(jnp.int32, w.shape, 1)
        gate_ref[...] = jnp.where(lane < N_GATES, hi,
                                  jnp.where(lane < 2 * N_GATES, lo, 0.0)).astype(BF16)


def _pack_win(w_in_t, *, tb):
    depth = w_in_t.shape[0]
    assert GATE_COL0 % tb == 0 and PROJ_W % tb == 0 and N_GATES == SUBLANES
    return pl.pallas_call(
        functools.partial(_pack_win_kernel, tb=tb),
        grid=(depth, PROJ_W // tb),
        in_specs=[
            pl.BlockSpec((None, tb, D_MODEL), lambda l, j: (l, j, 0)),
            pl.BlockSpec((None, N_GATES, D_MODEL),
                         lambda l, j: (l, (j + 1) * (tb // N_GATES), 0)),
        ],
        out_specs=[pl.BlockSpec((None, D_MODEL, tb), lambda l, j: (l, 0, j)),
                   pl.BlockSpec((None, D_MODEL, LANES), lambda l, j: (l, 0, 0))],
        out_shape=[jax.ShapeDtypeStruct((depth, D_MODEL, PROJ_W), BF16),
                   jax.ShapeDtypeStruct((depth, D_MODEL, LANES), BF16)],
        compiler_params=_params("arbitrary", "arbitrary"),
        name="pack_win",
    )(w_in_t, w_in_t)


def _inproj_kernel(h_ref, nw_ref, w_ref, wg_ref, proj_ref, gates_ref, hn_ref):
    @pl.when(pl.program_id(1) == 0)
    def _():
        hn = _rms_scale(h_ref[...], nw_ref[...]).astype(BF16)
        hn_ref[...] = hn
        gates_ref[...] = jnp.dot(hn, wg_ref[...], preferred_element_type=F32)

    proj_ref[...] = jnp.dot(hn_ref[...], w_ref[...],
                            preferred_element_type=F32).astype(BF16)


def _inproj(h, nw, w, wg, layer, *, tm, tn):
    m = h.shape[0]
    return pl.pallas_call(
        _inproj_kernel,
        grid=(m // tm, PROJ_W // tn),
        in_specs=[
            pl.BlockSpec((tm, D_MODEL), lambda i, j: (i, 0)),
            pl.BlockSpec((None, 1, D_MODEL), lambda i, j: (layer, 0, 0)),
            pl.BlockSpec((None, D_MODEL, tn), lambda i, j: (layer, 0, j)),
            pl.BlockSpec((None, D_MODEL, LANES), lambda i, j: (layer, 0, 0)),
        ],
        out_specs=[
            pl.BlockSpec((tm, tn), lambda i, j: (i, j)),
            pl.BlockSpec((tm, LANES), lambda i, j: (i, 0)),
        ],
        out_shape=[
            jax.ShapeDtypeStruct((m, PROJ_W), BF16),
            jax.ShapeDtypeStruct((m, LANES), F32),
        ],
        scratch_shapes=[pltpu.VMEM((tm, D_MODEL), BF16)],
        compiler_params=_params("arbitrary", "arbitrary"),
        name="inproj",
    )(h, nw, w, wg)


def _log_sigmoid(x):
    return jnp.minimum(x, 0.0) - jnp.log1p(jnp.exp(-jnp.abs(x)))


def _cumsum_rows(x, tri):
    hi = x.astype(BF16)
    r1 = x - hi.astype(F32)
    mid = r1.astype(BF16)
    lo = (r1 - mid.astype(F32)).astype(BF16)
    dot = functools.partial(jnp.dot, preferred_element_type=F32)
    return dot(tri, hi) + dot(tri, mid) + dot(tri, lo)


def _mixout_kernel(proj_ref, gates_ref, bias_ref, convw_ref, mnw_ref,
                   c0_ref, n0_ref, m0_ref, t0_ref, h_ref, wo_ref, nwf_ref,
                   hnew_ref, hf_ref, cn_ref, nn_ref, mn_ref, tn_ref,
                   c_scr, n_scr, m_scr, a_scr, y_prev, y_cur,
                   *, n_valid, n_chunks, n_steps):
    c = CHUNK
    s = pl.program_id(0)
    chunk = jnp.minimum(s, n_steps - 1)

    @pl.when(s == 0)
    def _():
        y_prev[...] = jnp.zeros_like(y_prev)

    @pl.when(jnp.logical_and(lax.rem(chunk, n_chunks) == 0, s < n_steps))
    def _():
        c_scr[...] = c0_ref[...]
        n_scr[...] = n0_ref[...]
        m_scr[...] = m0_ref[...]
        a_scr[0:SUBLANES, :] = t0_ref[...]

    hnew = h_ref[...] + jnp.dot(y_prev[...], wo_ref[...], preferred_element_type=F32)
    hnew_ref[...] = hnew
    hf_ref[...] = _rms_scale(hnew, nwf_ref[...]).astype(BF16)

    graw = gates_ref[...]
    g = graw + pltpu.roll(graw, LANES - N_GATES, axis=1) + bias_ref[...]
    capped = GATE_CAP * jnp.tanh(g / GATE_CAP)
    lane = lax.broadcasted_iota(jnp.int32, (c, LANES), 1)
    log_i = capped
    log_f = _log_sigmoid(capped)
    if n_valid < c:
        row_ok = lax.broadcasted_iota(jnp.int32, (c, LANES), 0) < n_valid
        log_i = jnp.where(row_ok, log_i, -jnp.inf)
        log_f = jnp.where(row_ok, log_f, 0.0)
    row = lax.broadcasted_iota(jnp.int32, (c, c), 0)
    col = lax.broadcasted_iota(jnp.int32, (c, c), 1)
    causal = row >= col
    tri = jnp.where(causal, 1.0, 0.0).astype(BF16)
    bcum = _cumsum_rows(jnp.where(lane >= HEADS, log_f, 0.0), tri)
    x_c = jnp.where(lane < HEADS, log_i, bcum)
    x_r = x_c.T

    for hd in range(HEADS):
        q = proj_ref[:, OFF_Q + hd * DQK:OFF_Q + (hd + 1) * DQK]
        k = proj_ref[:, OFF_K + hd * DQK:OFF_K + (hd + 1) * DQK]
        v = proj_ref[:, OFF_V + hd * DV:OFF_V + (hd + 1) * DV]
        og = proj_ref[:, OFF_OG + hd * DV:OFF_OG + (hd + 1) * DV].astype(F32)
        li_c = x_c[:, hd:hd + 1]
        b_c = x_c[:, HEADS + hd:HEADS + hd + 1]
        li_r = x_r[hd:hd + 1, :]
        b_r = x_r[HEADS + hd:HEADS + hd + 1, :]
        c_st = c_scr[hd]
        n_st = n_scr[hd]
        m_st = m_scr[hd][:, 0:1]

        dmat = jnp.where(causal, b_c - b_r + li_r, -jnp.inf)
        inter = b_c + m_st
        m_t = jnp.maximum(inter, jnp.max(dmat, axis=-1, keepdims=True))
        w_inter = jnp.exp(inter - m_t)
        sc = lax.dot_general(q, k, (((1,), (1,)), ((), ())),
                             preferred_element_type=F32) * QK_SCALE
        s_w = sc * jnp.exp(dmat - m_t)
        qc = jnp.dot(q, c_st.astype(BF16), preferred_element_type=F32) * QK_SCALE
        num = w_inter * qc + jnp.dot(s_w.astype(BF16), v, preferred_element_type=F32)
        qn = jnp.sum(q.astype(F32) * n_st, axis=-1, keepdims=True) * QK_SCALE
        den = w_inter * qn + jnp.sum(s_w, axis=-1, keepdims=True)
        hval = num / jnp.maximum(jnp.abs(den), jnp.exp(-m_t))

        b_end = b_c[c - 1:c, :]
        decay = b_end - b_c + li_c
        m_new = jnp.maximum(b_end + m_st, jnp.max(decay, axis=0, keepdims=True))
        w_old = jnp.exp(b_end + m_st - m_new)
        kw = k.astype(F32) * jnp.exp(decay - m_new)
        c_scr[hd] = w_old * c_st + jnp.dot(kw.T.astype(BF16), v,
                                           preferred_element_type=F32)
        n_scr[hd] = w_old * n_st + jnp.sum(kw, axis=0, keepdims=True)
        m_scr[hd] = jnp.broadcast_to(m_new, (1, LANES))

        hm = _rms_scale(hval, mnw_ref[:, hd * DV:(hd + 1) * DV])
        y_cur[:, hd * DV:(hd + 1) * DV] = (jax.nn.sigmoid(og) * hm).astype(BF16)

    u = proj_ref[:, OFF_U:OFF_U + CONV_W].astype(F32)
    gb = proj_ref[:, OFF_GB:OFF_GB + CONV_W].astype(F32)
    gc = proj_ref[:, OFF_GC:OFF_GC + CONV_W].astype(F32)
    a = gc * u
    a_scr[SUBLANES:SUBLANES + c, :] = a
    a1 = a_scr[SUBLANES - 1:SUBLANES - 1 + c, :]
    a2 = a_scr[SUBLANES - 2:SUBLANES - 2 + c, :]
    conv = a2 * convw_ref[0:1, :] + a1 * convw_ref[1:2, :] + a * convw_ref[2:3, :]
    y_cur[:, MLSTM_W:MLSTM_W + CONV_W] = (gb * conv).astype(BF16)
    a_scr[0:SUBLANES, :] = a_scr[n_valid:n_valid + SUBLANES, :]

    y_prev[...] = y_cur[...]

    @pl.when(s == n_steps - 1)
    def _():
        cn_ref[...] = c_scr[...]
        nn_ref[...] = n_scr[...]
        mn_ref[...] = m_scr[...]
        tn_ref[...] = a_scr[0:SUBLANES, :]


def _mixout(proj, gates, bias, convw, mnw, state, h, wo, nwf, layer, *, n_seq, n_valid):
    m = proj.shape[0]
    n_steps = m // CHUNK
    n_chunks = n_steps // n_seq
    mix_rows = lambda s: (jnp.minimum(s, n_steps - 1), 0)
    out_rows = lambda s: (jnp.maximum(s - 1, 0), 0)
    const2 = lambda s: (0, 0)
    const3 = lambda s: (0, 0, 0)
    per_layer = lambda s: (layer, 0, 0)
    state_specs = [
        pl.BlockSpec((HEADS, DQK, DV), const3),
        pl.BlockSpec((HEADS, 1, LANES), const3),
        pl.BlockSpec((HEADS, 1, LANES), const3),
        pl.BlockSpec((SUBLANES, CONV_W), const2),
    ]
    state_shapes = [
        jax.ShapeDtypeStruct((HEADS, DQK, DV), F32),
        jax.ShapeDtypeStruct((HEADS, 1, LANES), F32),
        jax.ShapeDtypeStruct((HEADS, 1, LANES), F32),
        jax.ShapeDtypeStruct((SUBLANES, CONV_W), F32),
    ]
    out = pl.pallas_call(
        functools.partial(_mixout_kernel, n_valid=n_valid, n_chunks=n_chunks,
                          n_steps=n_steps),
        grid=(n_steps + 1,),
        in_specs=[
            pl.BlockSpec((CHUNK, PROJ_W), mix_rows),
            pl.BlockSpec((CHUNK, LANES), mix_rows),
            pl.BlockSpec((None, 1, LANES), per_layer),
            pl.BlockSpec((None, SUBLANES, CONV_W), per_layer),
            pl.BlockSpec((None, 1, MLSTM_W), per_layer),
        ] + state_specs + [
            pl.BlockSpec((CHUNK, D_MODEL), out_rows),
            pl.BlockSpec((None, D_MODEL, D_MODEL), per_layer),
            pl.BlockSpec((None, 1, D_MODEL), per_layer),
        ],
        out_specs=[pl.BlockSpec((CHUNK, D_MODEL), out_rows),
                   pl.BlockSpec((CHUNK, D_MODEL), out_rows)] + state_specs,
        out_shape=[jax.ShapeDtypeStruct((m, D_MODEL), F32),
                   jax.ShapeDtypeStruct((m, D_MODEL), BF16)] + state_shapes,
        scratch_shapes=[
            pltpu.VMEM((HEADS, DQK, DV), F32),
            pltpu.VMEM((HEADS, 1, LANES), F32),
            pltpu.VMEM((HEADS, 1, LANES), F32),
            pltpu.VMEM((SUBLANES + CHUNK, CONV_W), F32),
            pltpu.VMEM((CHUNK, D_MODEL), BF16),
            pltpu.VMEM((CHUNK, D_MODEL), BF16),
        ],
        compiler_params=_params("arbitrary"),
        name="mixout",
    )(proj, gates, bias, convw, mnw, *state, h, wo, nwf)
    return out[0], out[1], tuple(out[2:])


def _ffn_kernel(hf_ref, h_hbm, wg_ref, wu_ref, wd_ref, nw_ref, out_ref, hbuf, sem,
                *, tm, norm_out):
    i = pl.program_id(0)
    k = pl.program_id(1)

    def residual_copy(tile):
        return pltpu.make_async_copy(h_hbm.at[pl.ds(tile * tm, tm), :], hbuf, sem)

    @pl.when(k == 0)
    def _():
        @pl.when(i == 0)
        def _():
            residual_copy(0).start()

        residual_copy(i).wait()
        out_ref[...] = hbuf[...]

        @pl.when(i + 1 < pl.num_programs(0))
        def _():
            residual_copy(i + 1).start()

    x = hf_ref[...]
    g = jnp.dot(x, wg_ref[...], preferred_element_type=F32)
    u = jnp.dot(x, wu_ref[...], preferred_element_type=F32)
    act = (g * jax.nn.sigmoid(g) * u).astype(BF16)
    out_ref[...] += jnp.dot(act, wd_ref[...], preferred_element_type=F32)

    if norm_out:
        @pl.when(k == pl.num_programs(1) - 1)
        def _():
            out_ref[...] = _rms_scale(out_ref[...], nw_ref[...])


def _ffn(hf, h, wg, wu, wd, layer, nw_out, *, tm, tf, norm_out):
    m = hf.shape[0]
    d_ff = wg.shape[2]
    rows = lambda i, k: (i, 0)
    return pl.pallas_call(
        functools.partial(_ffn_kernel, tm=tm, norm_out=norm_out),
        grid=(m // tm, d_ff // tf),
        in_specs=[
            pl.BlockSpec((tm, D_MODEL), rows),
            pl.BlockSpec(memory_space=pl.ANY),
            pl.BlockSpec((None, D_MODEL, tf), lambda i, k: (layer, 0, k)),
            pl.BlockSpec((None, D_MODEL, tf), lambda i, k: (layer, 0, k)),
            pl.BlockSpec((None, tf, D_MODEL), lambda i, k: (layer, k, 0)),
            pl.BlockSpec((1, D_MODEL), lambda i, k: (0, 0)),
        ],
        out_specs=pl.BlockSpec((tm, D_MODEL), rows),
        out_shape=jax.ShapeDtypeStruct((m, D_MODEL), F32),
        scratch_shapes=[pltpu.VMEM((tm, D_MODEL), F32),
                        pltpu.SemaphoreType.DMA(())],
        compiler_params=_params("arbitrary", "arbitrary"),
        name="ffn",
    )(hf, h, wg, wu, wd, nw_out)


def kernel(x, meta_tokens, norm_mix_w, w_in, b_gates, conv_w, mlstm_norm_w, w_out,
           norm_ffn_w, w_gate, w_up, w_down, norm_final_w):
    bsz, seq, _ = x.shape
    depth = w_in.shape[0]
    assert seq % CHUNK == 0 and N_META <= META_ROWS

    h_x = x.reshape(bsz * seq, D_MODEL)
    h_m = jnp.concatenate(
        [meta_tokens.astype(x.dtype), jnp.zeros((META_ROWS - N_META, D_MODEL), x.dtype)])

    w_main, w_gates = _pack_win(jnp.swapaxes(w_in, 1, 2), tb=512)
    wo = _cast_bf16(w_out, rb=512)
    wg = _cast_bf16(w_gate, rb=256)
    wu = _cast_bf16(w_up, rb=256)
    wd = _cast_bf16(w_down, rb=512)
    bias = jnp.pad(b_gates, ((0, 0), (0, LANES - N_GATES)))[:, None, :]
    convw = jnp.pad(conv_w, ((0, 0), (0, SUBLANES - CONV_K), (0, 0)))
    nw_mix = norm_mix_w[:, None, :]
    nw_ffn = norm_ffn_w[:, None, :]
    mnw = mlstm_norm_w[:, None, :]
    nw_final = norm_final_w[None, :]

    zero_state = (jnp.zeros((HEADS, DQK, DV), F32),
                  jnp.zeros((HEADS, 1, LANES), F32),
                  jnp.zeros((HEADS, 1, LANES), F32),
                  jnp.zeros((SUBLANES, CONV_W), F32))

    for l in range(depth):
        last = l == depth - 1
        proj, gates = _inproj(h_m, nw_mix, w_main, w_gates, l, tm=META_ROWS, tn=1024)
        h_m, hf, state = _mixout(proj, gates, bias, convw, mnw, zero_state,
                                 h_m, wo, nw_ffn, l, n_seq=1, n_valid=N_META)
        h_m = _ffn(hf, h_m, wg, wu, wd, l, nw_final, tm=META_ROWS, tf=512,
                   norm_out=False)

        proj, gates = _inproj(h_x, nw_mix, w_main, w_gates, l, tm=1024, tn=1024)
        h_x, hf, _ = _mixout(proj, gates, bias, convw, mnw, state,
                             h_x, wo, nw_ffn, l, n_seq=bsz, n_valid=CHUNK)
        h_x = _ffn(hf, h_x, wg, wu, wd, l, nw_final, tm=1024, tf=512, norm_out=last)

    return h_x.reshape(bsz, seq, D_MODEL)
```

```python
import functools
import math

import jax
import jax.numpy as jnp
from jax import lax
from jax.experimental import pallas as pl
from jax.experimental.pallas import tpu as pltpu

D_MODEL = 2048
N_META = 16
MLSTM_W = D_MODEL // 2
CONV_W = D_MODEL - MLSTM_W
HEADS = 4
DV = MLSTM_W // HEADS
DQK = DV // 2
QK_W = HEADS * DQK
CONV_K = 3
GATE_CAP = 15.0
EPS = 1e-6
QK_SCALE = DQK ** -0.5
LOG_QK_SCALE = math.log(QK_SCALE)

OFF_Q = 0
OFF_K = QK_W
OFF_V = 2 * QK_W
OFF_OG = OFF_V + MLSTM_W
OFF_U = OFF_OG + MLSTM_W
OFF_GB = OFF_U + CONV_W
OFF_GC = OFF_GB + CONV_W
PROJ_W = OFF_GC + CONV_W
GATE_COL0 = OFF_OG + MLSTM_W
N_GATES = 2 * HEADS
D_IN = PROJ_W + N_GATES

LANES = 128
SUBLANES = 8
CHUNK = 256
META_ROWS = CHUNK
VMEM_LIMIT = 56 * 1024 * 1024

F32 = jnp.float32
BF16 = jnp.bfloat16


def _params(*semantics):
    return pltpu.CompilerParams(dimension_semantics=semantics,
                                vmem_limit_bytes=VMEM_LIMIT)


def _rms_scale(x, w):
    ms = jnp.mean(x * x, axis=-1, keepdims=True)
    return x * lax.rsqrt(ms + EPS) * w


def _cast_kernel(x_ref, o_ref):
    o_ref[...] = x_ref[...].astype(BF16)


def _cast_bf16(w, *, rb):
    depth, rows, cols = w.shape
    spec = pl.BlockSpec((None, rb, cols), lambda l, i: (l, i, 0))
    return pl.pallas_call(
        _cast_kernel,
        grid=(depth, rows // rb),
        in_specs=[spec],
        out_specs=spec,
        out_shape=jax.ShapeDtypeStruct(w.shape, BF16),
        compiler_params=_params("arbitrary", "arbitrary"),
        name="cast_bf16",
    )(w)


def _pack_win_kernel(a_ref, b_ref, main_ref, gate_ref, *, tb):
    j = pl.program_id(1)
    first_shifted = GATE_COL0 // tb

    @pl.when(j < first_shifted)
    def _():
        main_ref[...] = a_ref[...].T.astype(BF16)

    @pl.when(j >= first_shifted)
    def _():
        x = jnp.concatenate([a_ref[N_GATES:, :], b_ref[...]], axis=0)
        main_ref[...] = x.T.astype(BF16)

    @pl.when(j == first_shifted)
    def _():
        w = a_ref[0:LANES, :].T
        hi = w.astype(BF16).astype(F32)
        lo = pltpu.roll(w - hi, N_GATES, axis=1)
        lane = lax.broadcasted_iota(jnp.int32, w.shape, 1)
        gate_ref[...] = jnp.where(lane < N_GATES, hi,
                                  jnp.where(lane < 2 * N_GATES, lo, 0.0)).astype(BF16)


def _pack_win(w_in_t, *, tb):
    depth = w_in_t.shape[0]
    assert GATE_COL0 % tb == 0 and PROJ_W % tb == 0 and N_GATES == SUBLANES
    return pl.pallas_call(
        functools.partial(_pack_win_kernel, tb=tb),
        grid=(depth, PROJ_W // tb),
        in_specs=[
            pl.BlockSpec((None, tb, D_MODEL), lambda l, j: (l, j, 0)),
            pl.BlockSpec((None, N_GATES, D_MODEL),
                         lambda l, j: (l, (j + 1) * (tb // N_GATES), 0)),
        ],
        out_specs=[pl.BlockSpec((None, D_MODEL, tb), lambda l, j: (l, 0, j)),
                   pl.BlockSpec((None, D_MODEL, LANES), lambda l, j: (l, 0, 0))],
        out_shape=[jax.ShapeDtypeStruct((depth, D_MODEL, PROJ_W), BF16),
                   jax.ShapeDtypeStruct((depth, D_MODEL, LANES), BF16)],
        compiler_params=_params("arbitrary", "arbitrary"),
        name="pack_win",
    )(w_in_t, w_in_t)


def _inproj_kernel(h_hbm, nw_ref, w_ref, wg_ref, proj_ref, gates_ref, hn_ref, hbuf, sem,
                   *, tm):
    i = pl.program_id(0)

    def h_copy(tile):
        return pltpu.make_async_copy(h_hbm.at[pl.ds(tile * tm, tm), :], hbuf, sem)

    @pl.when(pl.program_id(1) == 0)
    def _():
        @pl.when(i == 0)
        def _():
            h_copy(0).start()

        h_copy(i).wait()
        hn = _rms_scale(hbuf[...], nw_ref[...]).astype(BF16)
        hn_ref[...] = hn
        gates_ref[...] = jnp.dot(hn, wg_ref[...], preferred_element_type=F32)

        @pl.when(i + 1 < pl.num_programs(0))
        def _():
            h_copy(i + 1).start()

    proj_ref[...] = jnp.dot(hn_ref[...], w_ref[...],
                            preferred_element_type=F32).astype(BF16)


def _inproj(h, nw, w, wg, layer, *, tm, tn):
    m = h.shape[0]
    return pl.pallas_call(
        functools.partial(_inproj_kernel, tm=tm),
        grid=(m // tm, PROJ_W // tn),
        in_specs=[
            pl.BlockSpec(memory_space=pl.ANY),
            pl.BlockSpec((None, 1, D_MODEL), lambda i, j: (layer, 0, 0)),
            pl.BlockSpec((None, D_MODEL, tn), lambda i, j: (layer, 0, j)),
            pl.BlockSpec((None, D_MODEL, LANES), lambda i, j: (layer, 0, 0)),
        ],
        out_specs=[
            pl.BlockSpec((tm, tn), lambda i, j: (i, j)),
            pl.BlockSpec((tm, LANES), lambda i, j: (i, 0)),
        ],
        out_shape=[
            jax.ShapeDtypeStruct((m, PROJ_W), BF16),
            jax.ShapeDtypeStruct((m, LANES), F32),
        ],
        scratch_shapes=[pltpu.VMEM((tm, D_MODEL), BF16),
                        pltpu.VMEM((tm, D_MODEL), F32),
                        pltpu.SemaphoreType.DMA(())],
        compiler_params=_params("arbitrary", "arbitrary"),
        name="inproj",
    )(h, nw, w, wg)


def _log_sigmoid(x):
    return jnp.minimum(x, 0.0) - jnp.log1p(jnp.exp(-jnp.abs(x)))


def _cumsum_rows(x, tri):
    hi = x.astype(BF16)
    r1 = x - hi.astype(F32)
    mid = r1.astype(BF16)
    lo = (r1 - mid.astype(F32)).astype(BF16)
    dot = functools.partial(jnp.dot, preferred_element_type=F32)
    return dot(tri, hi) + dot(tri, mid) + dot(tri, lo)


def _mixout_kernel(proj_ref, gates_ref, bias_ref, convw_ref, mnw_ref,
                   c0_ref, n0_ref, m0_ref, t0_ref, h_ref, wo_ref, nwf_ref,
                   hnew_ref, hf_ref, cn_ref, nn_ref, mn_ref, tn_ref,
                   c_scr, n_scr, m_scr, a_scr, y_prev, y_cur,
                   *, n_valid, n_chunks, n_steps):
    c = CHUNK
    s = pl.program_id(0)
    chunk = jnp.minimum(s, n_steps - 1)

    @pl.when(s == 0)
    def _():
        y_prev[...] = jnp.zeros_like(y_prev)

    @pl.when(jnp.logical_and(lax.rem(chunk, n_chunks) == 0, s < n_steps))
    def _():
        c_scr[...] = c0_ref[...]
        n_scr[...] = n0_ref[...]
        m_scr[...] = m0_ref[...]
        a_scr[0:SUBLANES, :] = t0_ref[...]

    hnew = h_ref[...] + jnp.dot(y_prev[...], wo_ref[...], preferred_element_type=F32)
    hnew_ref[...] = hnew
    hf_ref[...] = _rms_scale(hnew, nwf_ref[...]).astype(BF16)

    graw = gates_ref[...]
    g = graw + pltpu.roll(graw, LANES - N_GATES, axis=1) + bias_ref[...]
    capped = GATE_CAP * jnp.tanh(g / GATE_CAP)
    lane = lax.broadcasted_iota(jnp.int32, (c, LANES), 1)
    log_i = capped
    log_f = _log_sigmoid(capped)
    if n_valid < c:
        row_ok = lax.broadcasted_iota(jnp.int32, (c, LANES), 0) < n_valid
        log_i = jnp.where(row_ok, log_i, -jnp.inf)
        log_f = jnp.where(row_ok, log_f, 0.0)
    row = lax.broadcasted_iota(jnp.int32, (c, c), 0)
    col = lax.broadcasted_iota(jnp.int32, (c, c), 1)
    causal = row >= col
    tri = jnp.where(causal, 1.0, 0.0).astype(BF16)
    bcum = _cumsum_rows(jnp.where(lane >= HEADS, log_f, 0.0), tri)
    x_c = jnp.where(lane < HEADS, log_i, bcum)
    x_r = x_c.T

    for hd in range(HEADS):
        q = proj_ref[:, OFF_Q + hd * DQK:OFF_Q + (hd + 1) * DQK]
        k = proj_ref[:, OFF_K + hd * DQK:OFF_K + (hd + 1) * DQK]
        v = proj_ref[:, OFF_V + hd * DV:OFF_V + (hd + 1) * DV]
        og = proj_ref[:, OFF_OG + hd * DV:OFF_OG + (hd + 1) * DV].astype(F32)
        li_c = x_c[:, hd:hd + 1]
        b_c = x_c[:, HEADS + hd:HEADS + hd + 1]
        li_r = x_r[hd:hd + 1, :]
        b_r = x_r[HEADS + hd:HEADS + hd + 1, :]
        c_st = c_scr[hd]
        n_st = n_scr[hd]
        m_st = m_scr[hd][:, 0:1]

        dmat = jnp.where(causal, b_c - b_r + li_r, -jnp.inf)
        inter = b_c + m_st
        m_t = jnp.maximum(inter, jnp.max(dmat, axis=-1, keepdims=True))
        w_inter = jnp.exp(inter - m_t)
        sc = lax.dot_general(q, k, (((1,), (1,)), ((), ())),
                             preferred_element_type=F32)
        s_w = sc * jnp.exp(dmat - (m_t - LOG_QK_SCALE))
        qc = jnp.dot(q, c_st.astype(BF16), preferred_element_type=F32)
        num = (w_inter * QK_SCALE) * qc + jnp.dot(s_w.astype(BF16), v,
                                                   preferred_element_type=F32)
        qn = jnp.sum(q.astype(F32) * n_st, axis=-1, keepdims=True) * QK_SCALE
        den = w_inter * qn + jnp.sum(s_w, axis=-1, keepdims=True)
        hval = num / jnp.maximum(jnp.abs(den), jnp.exp(-m_t))

        b_end = b_c[c - 1:c, :]
        decay = b_end - b_c + li_c
        m_new = jnp.maximum(b_end + m_st, jnp.max(decay, axis=0, keepdims=True))
        w_old = jnp.exp(b_end + m_st - m_new)
        kw = k.astype(F32) * jnp.exp(decay - m_new)
        c_scr[hd] = w_old * c_st + jnp.dot(kw.T.astype(BF16), v,
                                           preferred_element_type=F32)
        n_scr[hd] = w_old * n_st + jnp.sum(kw, axis=0, keepdims=True)
        m_scr[hd] = jnp.broadcast_to(m_new, (1, LANES))

        hm = _rms_scale(hval, mnw_ref[:, hd * DV:(hd + 1) * DV])
        y_cur[:, hd * DV:(hd + 1) * DV] = (jax.nn.sigmoid(og) * hm).astype(BF16)

    u = proj_ref[:, OFF_U:OFF_U + CONV_W].astype(F32)
    gb = proj_ref[:, OFF_GB:OFF_GB + CONV_W].astype(F32)
    gc = proj_ref[:, OFF_GC:OFF_GC + CONV_W].astype(F32)
    a = gc * u
    a_scr[SUBLANES:SUBLANES + c, :] = a
    a1 = a_scr[SUBLANES - 1:SUBLANES - 1 + c, :]
    a2 = a_scr[SUBLANES - 2:SUBLANES - 2 + c, :]
    conv = a2 * convw_ref[0:1, :] + a1 * convw_ref[1:2, :] + a * convw_ref[2:3, :]
    y_cur[:, MLSTM_W:MLSTM_W + CONV_W] = (gb * conv).astype(BF16)
    a_scr[0:SUBLANES, :] = a_scr[n_valid:n_valid + SUBLANES, :]

    y_prev[...] = y_cur[...]

    @pl.when(s == n_steps - 1)
    def _():
        cn_ref[...] = c_scr[...]
        nn_ref[...] = n_scr[...]
        mn_ref[...] = m_scr[...]
        tn_ref[...] = a_scr[0:SUBLANES, :]


def _mixout(proj, gates, bias, convw, mnw, state, h, wo, nwf, layer, *, n_seq, n_valid):
    m = proj.shape[0]
    n_steps = m // CHUNK
    n_chunks = n_steps // n_seq
    mix_rows = lambda s: (jnp.minimum(s, n_steps - 1), 0)
    out_rows = lambda s: (jnp.maximum(s - 1, 0), 0)
    const2 = lambda s: (0, 0)
    const3 = lambda s: (0, 0, 0)
    per_layer = lambda s: (layer, 0, 0)
    state_specs = [
        pl.BlockSpec((HEADS, DQK, DV), const3),
        pl.BlockSpec((HEADS, 1, LANES), const3),
        pl.BlockSpec((HEADS, 1, LANES), const3),
        pl.BlockSpec((SUBLANES, CONV_W), const2),
    ]
    state_shapes = [
        jax.ShapeDtypeStruct((HEADS, DQK, DV), F32),
        jax.ShapeDtypeStruct((HEADS, 1, LANES), F32),
        jax.ShapeDtypeStruct((HEADS, 1, LANES), F32),
        jax.ShapeDtypeStruct((SUBLANES, CONV_W), F32),
    ]
    out = pl.pallas_call(
        functools.partial(_mixout_kernel, n_valid=n_valid, n_chunks=n_chunks,
                          n_steps=n_steps),
        grid=(n_steps + 1,),
        in_specs=[
            pl.BlockSpec((CHUNK, PROJ_W), mix_rows),
            pl.BlockSpec((CHUNK, LANES), mix_rows),
            pl.BlockSpec((None, 1, LANES), per_layer),
            pl.BlockSpec((None, SUBLANES, CONV_W), per_layer),
            pl.BlockSpec((None, 1, MLSTM_W), per_layer),
        ] + state_specs + [
            pl.BlockSpec((CHUNK, D_MODEL), out_rows),
            pl.BlockSpec((None, D_MODEL, D_MODEL), per_layer),
            pl.BlockSpec((None, 1, D_MODEL), per_layer),
        ],
        out_specs=[pl.BlockSpec((CHUNK, D_MODEL), out_rows),
                   pl.BlockSpec((CHUNK, D_MODEL), out_rows)] + state_specs,
        out_shape=[jax.ShapeDtypeStruct((m, D_MODEL), F32),
                   jax.ShapeDtypeStruct((m, D_MODEL), BF16)] + state_shapes,
        scratch_shapes=[
            pltpu.VMEM((HEADS, DQK, DV), F32),
            pltpu.VMEM((HEADS, 1, LANES), F32),
            pltpu.VMEM((HEADS, 1, LANES), F32),
            pltpu.VMEM((SUBLANES + CHUNK, CONV_W), F32),
            pltpu.VMEM((CHUNK, D_MODEL), BF16),
            pltpu.VMEM((CHUNK, D_MODEL), BF16),
        ],
        compiler_params=_params("arbitrary"),
        name="mixout",
    )(proj, gates, bias, convw, mnw, *state, h, wo, nwf)
    return out[0], out[1], tuple(out[2:])


def _ffn_kernel(hf_ref, h_hbm, wg_ref, wu_ref, wd_ref, nw_ref, out_ref, hbuf, sem,
                *, tm, norm_out):
    i = pl.program_id(0)
    k = pl.program_id(1)

    def residual_copy(tile):
        return pltpu.make_async_copy(h_hbm.at[pl.ds(tile * tm, tm), :], hbuf, sem)

    @pl.when(k == 0)
    def _():
        @pl.when(i == 0)
        def _():
            residual_copy(0).start()

        residual_copy(i).wait()
        out_ref[...] = hbuf[...]

        @pl.when(i + 1 < pl.num_programs(0))
        def _():
            residual_copy(i + 1).start()

    x = hf_ref[...]
    g = jnp.dot(x, wg_ref[...], preferred_element_type=F32)
    u = jnp.dot(x, wu_ref[...], preferred_element_type=F32)
    act = (g * jax.nn.sigmoid(g) * u).astype(BF16)
    out_ref[...] += jnp.dot(act, wd_ref[...], preferred_element_type=F32)

    if norm_out:
        @pl.when(k == pl.num_programs(1) - 1)
        def _():
            out_ref[...] = _rms_scale(out_ref[...], nw_ref[...])


def _ffn(hf, h, wg, wu, wd, layer, nw_out, *, tm, tf, norm_out):
    m = hf.shape[0]
    d_ff = wg.shape[2]
    rows = lambda i, k: (i, 0)
    return pl.pallas_call(
        functools.partial(_ffn_kernel, tm=tm, norm_out=norm_out),
        grid=(m // tm, d_ff // tf),
        in_specs=[
            pl.BlockSpec((tm, D_MODEL), rows),
            pl.BlockSpec(memory_space=pl.ANY),
            pl.BlockSpec((None, D_MODEL, tf), lambda i, k: (layer, 0, k)),
            pl.BlockSpec((None, D_MODEL, tf), lambda i, k: (layer, 0, k)),
            pl.BlockSpec((None, tf, D_MODEL), lambda i, k: (layer, k, 0)),
            pl.BlockSpec((1, D_MODEL), lambda i, k: (0, 0)),
        ],
        out_specs=pl.BlockSpec((tm, D_MODEL), rows),
        out_shape=jax.ShapeDtypeStruct((m, D_MODEL), F32),
        scratch_shapes=[pltpu.VMEM((tm, D_MODEL), F32),
                        pltpu.SemaphoreType.DMA(())],
        compiler_params=_params("arbitrary", "arbitrary"),
        name="ffn",
    )(hf, h, wg, wu, wd, nw_out)


def kernel(x, meta_tokens, norm_mix_w, w_in, b_gates, conv_w, mlstm_norm_w, w_out,
           norm_ffn_w, w_gate, w_up, w_down, norm_final_w):
    bsz, seq, _ = x.shape
    depth = w_in.shape[0]
    assert seq % CHUNK == 0 and N_META <= META_ROWS

    h_x = x.reshape(bsz * seq, D_MODEL)
    h_m = jnp.concatenate(
        [meta_tokens.astype(x.dtype), jnp.zeros((META_ROWS - N_META, D_MODEL), x.dtype)])

    w_main, w_gates = _pack_win(jnp.swapaxes(w_in, 1, 2), tb=512)
    wo = _cast_bf16(w_out, rb=512)
    wg = _cast_bf16(w_gate, rb=256)
    wu = _cast_bf16(w_up, rb=256)
    wd = _cast_bf16(w_down, rb=512)
    bias = jnp.pad(b_gates, ((0, 0), (0, LANES - N_GATES)))[:, None, :]
    convw = jnp.pad(conv_w, ((0, 0), (0, SUBLANES - CONV_K), (0, 0)))
    nw_mix = norm_mix_w[:, None, :]
    nw_ffn = norm_ffn_w[:, None, :]
    mnw = mlstm_norm_w[:, None, :]
    nw_final = norm_final_w[None, :]

    zero_state = (jnp.zeros((HEADS, DQK, DV), F32),
                  jnp.zeros((HEADS, 1, LANES), F32),
                  jnp.zeros((HEADS, 1, LANES), F32),
                  jnp.zeros((SUBLANES, CONV_W), F32))

    for l in range(depth):
        last = l == depth - 1
        proj, gates = _inproj(h_m, nw_mix, w_main, w_gates, l, tm=META_ROWS, tn=1024)
        h_m, hf, state = _mixout(proj, gates, bias, convw, mnw, zero_state,
                                 h_m, wo, nw_ffn, l, n_seq=1, n_valid=N_META)
        h_m = _ffn(hf, h_m, wg, wu, wd, l, nw_final, tm=META_ROWS, tf=512,
                   norm_out=False)

        proj, gates = _inproj(h_x, nw_mix, w_main, w_gates, l, tm=1024, tn=2048)
        h_x, hf, _ = _mixout(proj, gates, bias, convw, mnw, state,
                             h_x, wo, nw_ffn, l, n_seq=bsz, n_valid=CHUNK)
        h_x = _ffn(hf, h_x, wg, wu, wd, l, nw_final, tm=1024, tf=512, norm_out=last)

    return h_x.reshape(bsz, seq, D_MODEL)
```

```python
import functools
import math

import jax
import jax.numpy as jnp
from jax import lax
from jax.experimental import pallas as pl
from jax.experimental.pallas import tpu as pltpu

D_MODEL = 2048
N_META = 16
MLSTM_W = D_MODEL // 2
CONV_W = D_MODEL - MLSTM_W
HEADS = 4
DV = MLSTM_W // HEADS
DQK = DV // 2
QK_W = HEADS * DQK
CONV_K = 3
GATE_CAP = 15.0
EPS = 1e-6
QK_SCALE = DQK ** -0.5
LOG_QK_SCALE = math.log(QK_SCALE)

OFF_Q = 0
OFF_K = QK_W
OFF_V = 2 * QK_W
OFF_OG = OFF_V + MLSTM_W
OFF_U = OFF_OG + MLSTM_W
OFF_GB = OFF_U + CONV_W
OFF_GC = OFF_GB + CONV_W
PROJ_W = OFF_GC + CONV_W
GATE_COL0 = OFF_OG + MLSTM_W
N_GATES = 2 * HEADS
D_IN = PROJ_W + N_GATES

LANES = 128
SUBLANES = 8
CHUNK = 256
OUT_BLOCK = D_MODEL // HEADS
META_ROWS = CHUNK
VMEM_LIMIT = 56 * 1024 * 1024

F32 = jnp.float32
BF16 = jnp.bfloat16


def _params(*semantics):
    return pltpu.CompilerParams(dimension_semantics=semantics,
                                vmem_limit_bytes=VMEM_LIMIT)


def _rms_scale(x, w):
    ms = jnp.mean(x * x, axis=-1, keepdims=True)
    return x * lax.rsqrt(ms + EPS) * w


def _cast_kernel(x_ref, o_ref):
    o_ref[...] = x_ref[...].astype(BF16)


def _cast_bf16(w, *, rb):
    depth, rows, cols = w.shape
    spec = pl.BlockSpec((None, rb, cols), lambda l, i: (l, i, 0))
    return pl.pallas_call(
        _cast_kernel,
        grid=(depth, rows // rb),
        in_specs=[spec],
        out_specs=spec,
        out_shape=jax.ShapeDtypeStruct(w.shape, BF16),
        compiler_params=_params("arbitrary", "arbitrary"),
        name="cast_bf16",
    )(w)


def _pack_win_kernel(a_ref, b_ref, main_ref, gate_ref, *, tb):
    j = pl.program_id(1)
    first_shifted = GATE_COL0 // tb

    @pl.when(j < first_shifted)
    def _():
        main_ref[...] = a_ref[...].T.astype(BF16)

    @pl.when(j >= first_shifted)
    def _():
        x = jnp.concatenate([a_ref[N_GATES:, :], b_ref[...]], axis=0)
        main_ref[...] = x.T.astype(BF16)

    @pl.when(j == first_shifted)
    def _():
        w = a_ref[0:LANES, :].T
        hi = w.astype(BF16).astype(F32)
        lo = pltpu.roll(w - hi, N_GATES, axis=1)
        lane = lax.broadcasted_iota(jnp.int32, w.shape, 1)
        gate_ref[...] = jnp.where(lane < N_GATES, hi,
                                  jnp.where(lane < 2 * N_GATES, lo, 0.0)).astype(BF16)


def _pack_win(w_in_t, *, tb):
    depth = w_in_t.shape[0]
    assert GATE_COL0 % tb == 0 and PROJ_W % tb == 0 and N_GATES == SUBLANES
    return pl.pallas_call(
        functools.partial(_pack_win_kernel, tb=tb),
        grid=(depth, PROJ_W // tb),
        in_specs=[
            pl.BlockSpec((None, tb, D_MODEL), lambda l, j: (l, j, 0)),
            pl.BlockSpec((None, N_GATES, D_MODEL),
                         lambda l, j: (l, (j + 1) * (tb // N_GATES), 0)),
        ],
        out_specs=[pl.BlockSpec((None, D_MODEL, tb), lambda l, j: (l, 0, j)),
                   pl.BlockSpec((None, D_MODEL, LANES), lambda l, j: (l, 0, 0))],
        out_shape=[jax.ShapeDtypeStruct((depth, D_MODEL, PROJ_W), BF16),
                   jax.ShapeDtypeStruct((depth, D_MODEL, LANES), BF16)],
        compiler_params=_params("arbitrary", "arbitrary"),
        name="pack_win",
    )(w_in_t, w_in_t)


def _inproj_kernel(h_hbm, nw_ref, w_ref, wg_ref, proj_ref, gates_ref, hn_ref, hbuf, sem,
                   *, tm):
    i = pl.program_id(0)

    def h_copy(tile):
        return pltpu.make_async_copy(h_hbm.at[pl.ds(tile * tm, tm), :], hbuf, sem)

    @pl.when(pl.program_id(1) == 0)
    def _():
        @pl.when(i == 0)
        def _():
            h_copy(0).start()

        h_copy(i).wait()
        hn = _rms_scale(hbuf[...], nw_ref[...]).astype(BF16)
        hn_ref[...] = hn
        gates_ref[...] = jnp.dot(hn, wg_ref[...], preferred_element_type=F32)

        @pl.when(i + 1 < pl.num_programs(0))
        def _():
            h_copy(i + 1).start()

    proj_ref[...] = jnp.dot(hn_ref[...], w_ref[...],
                            preferred_element_type=F32).astype(BF16)


def _inproj(h, nw, w, wg, layer, *, tm, tn):
    m = h.shape[0]
    return pl.pallas_call(
        functools.partial(_inproj_kernel, tm=tm),
        grid=(m // tm, PROJ_W // tn),
        in_specs=[
            pl.BlockSpec(memory_space=pl.ANY),
            pl.BlockSpec((None, 1, D_MODEL), lambda i, j: (layer, 0, 0)),
            pl.BlockSpec((None, D_MODEL, tn), lambda i, j: (layer, 0, j)),
            pl.BlockSpec((None, D_MODEL, LANES), lambda i, j: (layer, 0, 0)),
        ],
        out_specs=[
            pl.BlockSpec((tm, tn), lambda i, j: (i, j)),
            pl.BlockSpec((tm, LANES), lambda i, j: (i, 0)),
        ],
        out_shape=[
            jax.ShapeDtypeStruct((m, PROJ_W), BF16),
            jax.ShapeDtypeStruct((m, LANES), F32),
        ],
        scratch_shapes=[pltpu.VMEM((tm, D_MODEL), BF16),
                        pltpu.VMEM((tm, D_MODEL), F32),
                        pltpu.SemaphoreType.DMA(())],
        compiler_params=_params("arbitrary", "arbitrary"),
        name="inproj",
    )(h, nw, w, wg)


def _log_sigmoid(x):
    return jnp.minimum(x, 0.0) - jnp.log1p(jnp.exp(-jnp.abs(x)))


def _cumsum_rows(x, tri):
    hi = x.astype(BF16)
    r1 = x - hi.astype(F32)
    mid = r1.astype(BF16)
    lo = (r1 - mid.astype(F32)).astype(BF16)
    dot = functools.partial(jnp.dot, preferred_element_type=F32)
    return dot(tri, hi) + dot(tri, mid) + dot(tri, lo)


def _mixout_kernel(proj_ref, gates_ref, bias_ref, convw_ref, mnw_ref,
                   c0_ref, n0_ref, m0_ref, t0_ref, h_ref, wo_ref, nwf_ref,
                   hnew_ref, hf_ref, cn_ref, nn_ref, mn_ref, tn_ref,
                   c_scr, n_scr, m_scr, a_scr, y_prev, y_cur,
                   *, n_valid, n_chunks, n_steps):
    c = CHUNK
    s = pl.program_id(0)
    chunk = jnp.minimum(s, n_steps - 1)

    @pl.when(s == 0)
    def _():
        y_prev[...] = jnp.zeros_like(y_prev)

    @pl.when(jnp.logical_and(lax.rem(chunk, n_chunks) == 0, s < n_steps))
    def _():
        c_scr[...] = c0_ref[...]
        n_scr[...] = n0_ref[...]
        m_scr[...] = m0_ref[...]
        a_scr[0:SUBLANES, :] = t0_ref[...]

    graw = gates_ref[...]
    g = graw + pltpu.roll(graw, LANES - N_GATES, axis=1) + bias_ref[...]
    capped = GATE_CAP * jnp.tanh(g / GATE_CAP)
    lane = lax.broadcasted_iota(jnp.int32, (c, LANES), 1)
    log_i = capped
    log_f = _log_sigmoid(capped)
    if n_valid < c:
        row_ok = lax.broadcasted_iota(jnp.int32, (c, LANES), 0) < n_valid
        log_i = jnp.where(row_ok, log_i, -jnp.inf)
        log_f = jnp.where(row_ok, log_f, 0.0)
    row = lax.broadcasted_iota(jnp.int32, (c, c), 0)
    col = lax.broadcasted_iota(jnp.int32, (c, c), 1)
    causal = row >= col
    tri = jnp.where(causal, 1.0, 0.0).astype(BF16)
    bcum = _cumsum_rows(jnp.where(lane >= HEADS, log_f, 0.0), tri)
    x_c = jnp.where(lane < HEADS, log_i, bcum)
    x_r = x_c.T

    def out_block(idx):
        n0 = idx * OUT_BLOCK
        return jnp.dot(y_prev[...], wo_ref[:, n0:n0 + OUT_BLOCK],
                       preferred_element_type=F32)

    out_parts = []
    for hd in range(HEADS):
        out_parts.append(out_block(hd))
        q = proj_ref[:, OFF_Q + hd * DQK:OFF_Q + (hd + 1) * DQK]
        k = proj_ref[:, OFF_K + hd * DQK:OFF_K + (hd + 1) * DQK]
        v = proj_ref[:, OFF_V + hd * DV:OFF_V + (hd + 1) * DV]
        og = proj_ref[:, OFF_OG + hd * DV:OFF_OG + (hd + 1) * DV].astype(F32)
        li_c = x_c[:, hd:hd + 1]
        b_c = x_c[:, HEADS + hd:HEADS + hd + 1]
        li_r = x_r[hd:hd + 1, :]
        b_r = x_r[HEADS + hd:HEADS + hd + 1, :]
        c_st = c_scr[hd]
        n_st = n_scr[hd]
        m_st = m_scr[hd][:, 0:1]

        dmat = jnp.where(causal, b_c - b_r + li_r, -jnp.inf)
        inter = b_c + m_st
        m_t = jnp.maximum(inter, jnp.max(dmat, axis=-1, keepdims=True))
        w_inter = jnp.exp(inter - m_t)
        sc = lax.dot_general(q, k, (((1,), (1,)), ((), ())),
                             preferred_element_type=F32)
        s_w = sc * jnp.exp(dmat - (m_t - LOG_QK_SCALE))
        qc = jnp.dot(q, c_st.astype(BF16), preferred_element_type=F32)
        num = (w_inter * QK_SCALE) * qc + jnp.dot(s_w.astype(BF16), v,
                                                   preferred_element_type=F32)
        qn = jnp.sum(q.astype(F32) * n_st, axis=-1, keepdims=True) * QK_SCALE
        den = w_inter * qn + jnp.sum(s_w, axis=-1, keepdims=True)
        hval = num / jnp.maximum(jnp.abs(den), jnp.exp(-m_t))

        b_end = b_c[c - 1:c, :]
        decay = b_end - b_c + li_c
        m_new = jnp.maximum(b_end + m_st, jnp.max(decay, axis=0, keepdims=True))
        w_old = jnp.exp(b_end + m_st - m_new)
        kw = k.astype(F32) * jnp.exp(decay - m_new)
        c_scr[hd] = w_old * c_st + jnp.dot(kw.T.astype(BF16), v,
                                           preferred_element_type=F32)
        n_scr[hd] = w_old * n_st + jnp.sum(kw, axis=0, keepdims=True)
        m_scr[hd] = jnp.broadcast_to(m_new, (1, LANES))

        hm = _rms_scale(hval, mnw_ref[:, hd * DV:(hd + 1) * DV])
        y_cur[:, hd * DV:(hd + 1) * DV] = (jax.nn.sigmoid(og) * hm).astype(BF16)

    u = proj_ref[:, OFF_U:OFF_U + CONV_W].astype(F32)
    gb = proj_ref[:, OFF_GB:OFF_GB + CONV_W].astype(F32)
    gc = proj_ref[:, OFF_GC:OFF_GC + CONV_W].astype(F32)
    a = gc * u
    a_scr[SUBLANES:SUBLANES + c, :] = a
    a1 = a_scr[SUBLANES - 1:SUBLANES - 1 + c, :]
    a2 = a_scr[SUBLANES - 2:SUBLANES - 2 + c, :]
    conv = a2 * convw_ref[0:1, :] + a1 * convw_ref[1:2, :] + a * convw_ref[2:3, :]
    y_cur[:, MLSTM_W:MLSTM_W + CONV_W] = (gb * conv).astype(BF16)
    a_scr[0:SUBLANES, :] = a_scr[n_valid:n_valid + SUBLANES, :]

    hnew = h_ref[...] + jnp.concatenate(out_parts, axis=1)
    hnew_ref[...] = hnew
    hf_ref[...] = _rms_scale(hnew, nwf_ref[...]).astype(BF16)

    y_prev[...] = y_cur[...]

    @pl.when(s == n_steps - 1)
    def _():
        cn_ref[...] = c_scr[...]
        nn_ref[...] = n_scr[...]
        mn_ref[...] = m_scr[...]
        tn_ref[...] = a_scr[0:SUBLANES, :]


def _mixout(proj, gates, bias, convw, mnw, state, h, wo, nwf, layer, *, n_seq, n_valid):
    m = proj.shape[0]
    n_steps = m // CHUNK
    n_chunks = n_steps // n_seq
    mix_rows = lambda s: (jnp.minimum(s, n_steps - 1), 0)
    out_rows = lambda s: (jnp.maximum(s - 1, 0), 0)
    const2 = lambda s: (0, 0)
    const3 = lambda s: (0, 0, 0)
    per_layer = lambda s: (layer, 0, 0)
    state_specs = [
        pl.BlockSpec((HEADS, DQK, DV), const3),
        pl.BlockSpec((HEADS, 1, LANES), const3),
        pl.BlockSpec((HEADS, 1, LANES), const3),
        pl.BlockSpec((SUBLANES, CONV_W), const2),
    ]
    state_shapes = [
        jax.ShapeDtypeStruct((HEADS, DQK, DV), F32),
        jax.ShapeDtypeStruct((HEADS, 1, LANES), F32),
        jax.ShapeDtypeStruct((HEADS, 1, LANES), F32),
        jax.ShapeDtypeStruct((SUBLANES, CONV_W), F32),
    ]
    out = pl.pallas_call(
        functools.partial(_mixout_kernel, n_valid=n_valid, n_chunks=n_chunks,
                          n_steps=n_steps),
        grid=(n_steps + 1,),
        in_specs=[
            pl.BlockSpec((CHUNK, PROJ_W), mix_rows),
            pl.BlockSpec((CHUNK, LANES), mix_rows),
            pl.BlockSpec((None, 1, LANES), per_layer),
            pl.BlockSpec((None, SUBLANES, CONV_W), per_layer),
            pl.BlockSpec((None, 1, MLSTM_W), per_layer),
        ] + state_specs + [
            pl.BlockSpec((CHUNK, D_MODEL), out_rows),
            pl.BlockSpec((None, D_MODEL, D_MODEL), per_layer),
            pl.BlockSpec((None, 1, D_MODEL), per_layer),
        ],
        out_specs=[pl.BlockSpec((CHUNK, D_MODEL), out_rows),
                   pl.BlockSpec((CHUNK, D_MODEL), out_rows)] + state_specs,
        out_shape=[jax.ShapeDtypeStruct((m, D_MODEL), F32),
                   jax.ShapeDtypeStruct((m, D_MODEL), BF16)] + state_shapes,
        scratch_shapes=[
            pltpu.VMEM((HEADS, DQK, DV), F32),
            pltpu.VMEM((HEADS, 1, LANES), F32),
            pltpu.VMEM((HEADS, 1, LANES), F32),
            pltpu.VMEM((SUBLANES + CHUNK, CONV_W), F32),
            pltpu.VMEM((CHUNK, D_MODEL), BF16),
            pltpu.VMEM((CHUNK, D_MODEL), BF16),
        ],
        compiler_params=_params("arbitrary"),
        name="mixout",
    )(proj, gates, bias, convw, mnw, *state, h, wo, nwf)
    return out[0], out[1], tuple(out[2:])


def _ffn_kernel(hf_ref, h_hbm, wg_ref, wu_ref, wd_ref, nw_ref, out_ref, hbuf, sem,
                *, tm, norm_out):
    i = pl.program_id(0)
    k = pl.program_id(1)

    def residual_copy(tile):
        return pltpu.make_async_copy(h_hbm.at[pl.ds(tile * tm, tm), :], hbuf, sem)

    @pl.when(k == 0)
    def _():
        @pl.when(i == 0)
        def _():
            residual_copy(0).start()

        residual_copy(i).wait()
        out_ref[...] = hbuf[...]

        @pl.when(i + 1 < pl.num_programs(0))
        def _():
            residual_copy(i + 1).start()

    x = hf_ref[...]
    g = jnp.dot(x, wg_ref[...], preferred_element_type=F32)
    u = jnp.dot(x, wu_ref[...], preferred_element_type=F32)
    act = (g * jax.nn.sigmoid(g) * u).astype(BF16)
    out_ref[...] += jnp.dot(act, wd_ref[...], preferred_element_type=F32)

    if norm_out:
        @pl.when(k == pl.num_programs(1) - 1)
        def _():
            out_ref[...] = _rms_scale(out_ref[...], nw_ref[...])


def _ffn(hf, h, wg, wu, wd, layer, nw_out, *, tm, tf, norm_out):
    m = hf.shape[0]
    d_ff = wg.shape[2]
    rows = lambda i, k: (i, 0)
    return pl.pallas_call(
        functools.partial(_ffn_kernel, tm=tm, norm_out=norm_out),
        grid=(m // tm, d_ff // tf),
        in_specs=[
            pl.BlockSpec((tm, D_MODEL), rows),
            pl.BlockSpec(memory_space=pl.ANY),
            pl.BlockSpec((None, D_MODEL, tf), lambda i, k: (layer, 0, k)),
            pl.BlockSpec((None, D_MODEL, tf), lambda i, k: (layer, 0, k)),
            pl.BlockSpec((None, tf, D_MODEL), lambda i, k: (layer, k, 0)),
            pl.BlockSpec((1, D_MODEL), lambda i, k: (0, 0)),
        ],
        out_specs=pl.BlockSpec((tm, D_MODEL), rows),
        out_shape=jax.ShapeDtypeStruct((m, D_MODEL), F32),
        scratch_shapes=[pltpu.VMEM((tm, D_MODEL), F32),
                        pltpu.SemaphoreType.DMA(())],
        compiler_params=_params("arbitrary", "arbitrary"),
        name="ffn",
    )(hf, h, wg, wu, wd, nw_out)


def kernel(x, meta_tokens, norm_mix_w, w_in, b_gates, conv_w, mlstm_norm_w, w_out,
           norm_ffn_w, w_gate, w_up, w_down, norm_final_w):
    bsz, seq, _ = x.shape
    depth = w_in.shape[0]
    assert seq % CHUNK == 0 and N_META <= META_ROWS

    h_x = x.reshape(bsz * seq, D_MODEL)
    h_m = jnp.concatenate(
        [meta_tokens.astype(x.dtype), jnp.zeros((META_ROWS - N_META, D_MODEL), x.dtype)])

    w_main, w_gates = _pack_win(jnp.swapaxes(w_in, 1, 2), tb=512)
    wo = _cast_bf16(w_out, rb=512)
    wg = _cast_bf16(w_gate, rb=256)
    wu = _cast_bf16(w_up, rb=256)
    wd = _cast_bf16(w_down, rb=512)
    bias = jnp.pad(b_gates, ((0, 0), (0, LANES - N_GATES)))[:, None, :]
    convw = jnp.pad(conv_w, ((0, 0), (0, SUBLANES - CONV_K), (0, 0)))
    nw_mix = norm_mix_w[:, None, :]
    nw_ffn = norm_ffn_w[:, None, :]
    mnw = mlstm_norm_w[:, None, :]
    nw_final = norm_final_w[None, :]

    zero_state = (jnp.zeros((HEADS, DQK, DV), F32),
                  jnp.zeros((HEADS, 1, LANES), F32),
                  jnp.zeros((HEADS, 1, LANES), F32),
                  jnp.zeros((SUBLANES, CONV_W), F32))

    for l in range(depth):
        last = l == depth - 1
        proj, gates = _inproj(h_m, nw_mix, w_main, w_gates, l, tm=META_ROWS, tn=1024)
        h_m, hf, state = _mixout(proj, gates, bias, convw, mnw, zero_state,
                                 h_m, wo, nw_ffn, l, n_seq=1, n_valid=N_META)
        h_m = _ffn(hf, h_m, wg, wu, wd, l, nw_final, tm=META_ROWS, tf=512,
                   norm_out=False)

        proj, gates = _inproj(h_x, nw_mix, w_main, w_gates, l, tm=1024, tn=2048)
        h_x, hf, _ = _mixout(proj, gates, bias, convw, mnw, state,
                             h_x, wo, nw_ffn, l, n_seq=bsz, n_valid=CHUNK)
        h_x = _ffn(hf, h_x, wg, wu, wd, l, nw_final, tm=1024, tf=512, norm_out=last)

    return h_x.reshape(bsz, seq, D_MODEL)
```

```python
import functools
import math

import jax
import jax.numpy as jnp
from jax import lax
from jax.experimental import pallas as pl
from jax.experimental.pallas import tpu as pltpu

D_MODEL = 2048
N_META = 16
MLSTM_W = D_MODEL // 2
CONV_W = D_MODEL - MLSTM_W
HEADS = 4
DV = MLSTM_W // HEADS
DQK = DV // 2
QK_W = HEADS * DQK
CONV_K = 3
GATE_CAP = 15.0
EPS = 1e-6
QK_SCALE = DQK ** -0.5
LOG_QK_SCALE = math.log(QK_SCALE)

OFF_Q = 0
OFF_K = QK_W
OFF_V = 2 * QK_W
OFF_OG = OFF_V + MLSTM_W
OFF_U = OFF_OG + MLSTM_W
OFF_GB = OFF_U + CONV_W
OFF_GC = OFF_GB + CONV_W
PROJ_W = OFF_GC + CONV_W
GATE_COL0 = OFF_OG + MLSTM_W
N_GATES = 2 * HEADS
D_IN = PROJ_W + N_GATES

LANES = 128
SUBLANES = 8
CHUNK = 256
OUT_BLOCK = D_MODEL // HEADS
META_ROWS = CHUNK
VMEM_LIMIT = 56 * 1024 * 1024

F32 = jnp.float32
BF16 = jnp.bfloat16


def _params(*semantics):
    return pltpu.CompilerParams(dimension_semantics=semantics,
                                vmem_limit_bytes=VMEM_LIMIT)


def _rms_scale(x, w):
    ms = jnp.mean(x * x, axis=-1, keepdims=True)
    return x * lax.rsqrt(ms + EPS) * w


def _cast_kernel(x_ref, o_ref):
    o_ref[...] = x_ref[...].astype(BF16)


def _cast_bf16(w, *, rb, n_layers=None):
    depth, rows, cols = w.shape
    n_layers = depth if n_layers is None else n_layers
    spec = pl.BlockSpec((None, rb, cols), lambda l, i: (l, i, 0))
    return pl.pallas_call(
        _cast_kernel,
        grid=(n_layers, rows // rb),
        in_specs=[spec],
        out_specs=spec,
        out_shape=jax.ShapeDtypeStruct((n_layers, rows, cols), BF16),
        compiler_params=_params("arbitrary", "arbitrary"),
        name="cast_bf16",
    )(w)


def _pack_win_kernel(a_ref, b_ref, main_ref, gate_ref, *, tb):
    j = pl.program_id(1)
    first_shifted = GATE_COL0 // tb

    @pl.when(j < first_shifted)
    def _():
        main_ref[...] = a_ref[...].T.astype(BF16)

    @pl.when(j >= first_shifted)
    def _():
        x = jnp.concatenate([a_ref[N_GATES:, :], b_ref[...]], axis=0)
        main_ref[...] = x.T.astype(BF16)

    @pl.when(j == first_shifted)
    def _():
        w = a_ref[0:LANES, :].T
        hi = w.astype(BF16).astype(F32)
        lo = pltpu.roll(w - hi, N_GATES, axis=1)
        lane = lax.broadcasted_iota(jnp.int32, w.shape, 1)
        gate_ref[...] = jnp.where(lane < N_GATES, hi,
                                  jnp.where(lane < 2 * N_GATES, lo, 0.0)).astype(BF16)


def _pack_win(w_in_t, *, tb):
    depth = w_in_t.shape[0]
    assert GATE_COL0 % tb == 0 and PROJ_W % tb == 0 and N_GATES == SUBLANES
    return pl.pallas_call(
        functools.partial(_pack_win_kernel, tb=tb),
        grid=(depth, PROJ_W // tb),
        in_specs=[
            pl.BlockSpec((None, tb, D_MODEL), lambda l, j: (l, j, 0)),
            pl.BlockSpec((None, N_GATES, D_MODEL),
                         lambda l, j: (l, (j + 1) * (tb // N_GATES), 0)),
        ],
        out_specs=[pl.BlockSpec((None, D_MODEL, tb), lambda l, j: (l, 0, j)),
                   pl.BlockSpec((None, D_MODEL, LANES), lambda l, j: (l, 0, 0))],
        out_shape=[jax.ShapeDtypeStruct((depth, D_MODEL, PROJ_W), BF16),
                   jax.ShapeDtypeStruct((depth, D_MODEL, LANES), BF16)],
        compiler_params=_params("arbitrary", "arbitrary"),
        name="pack_win",
    )(w_in_t, w_in_t)


def _inproj_kernel(h_hbm, nw_ref, w_ref, wg_ref, proj_ref, gates_ref, hn_ref, hbuf, sem,
                   *, tm):
    i = pl.program_id(0)

    def h_copy(tile):
        return pltpu.make_async_copy(h_hbm.at[pl.ds(tile * tm, tm), :], hbuf, sem)

    @pl.when(pl.program_id(1) == 0)
    def _():
        @pl.when(i == 0)
        def _():
            h_copy(0).start()

        h_copy(i).wait()
        hn = _rms_scale(hbuf[...], nw_ref[...]).astype(BF16)
        hn_ref[...] = hn
        gates_ref[...] = jnp.dot(hn, wg_ref[...], preferred_element_type=F32)

        @pl.when(i + 1 < pl.num_programs(0))
        def _():
            h_copy(i + 1).start()

    proj_ref[...] = jnp.dot(hn_ref[...], w_ref[...],
                            preferred_element_type=F32).astype(BF16)


def _inproj(h, nw, w, wg, layer, *, tm, tn):
    m = h.shape[0]
    return pl.pallas_call(
        functools.partial(_inproj_kernel, tm=tm),
        grid=(m // tm, PROJ_W // tn),
        in_specs=[
            pl.BlockSpec(memory_space=pl.ANY),
            pl.BlockSpec((None, 1, D_MODEL), lambda i, j: (layer, 0, 0)),
            pl.BlockSpec((None, D_MODEL, tn), lambda i, j: (layer, 0, j)),
            pl.BlockSpec((None, D_MODEL, LANES), lambda i, j: (layer, 0, 0)),
        ],
        out_specs=[
            pl.BlockSpec((tm, tn), lambda i, j: (i, j)),
            pl.BlockSpec((tm, LANES), lambda i, j: (i, 0)),
        ],
        out_shape=[
            jax.ShapeDtypeStruct((m, PROJ_W), BF16),
            jax.ShapeDtypeStruct((m, LANES), F32),
        ],
        scratch_shapes=[pltpu.VMEM((tm, D_MODEL), BF16),
                        pltpu.VMEM((tm, D_MODEL), F32),
                        pltpu.SemaphoreType.DMA(())],
        compiler_params=_params("arbitrary", "arbitrary"),
        name="inproj",
    )(h, nw, w, wg)


def _log_sigmoid(x):
    return jnp.minimum(x, 0.0) - jnp.log1p(jnp.exp(-jnp.abs(x)))


def _cumsum_rows(x, tri):
    hi = x.astype(BF16)
    r1 = x - hi.astype(F32)
    mid = r1.astype(BF16)
    lo = (r1 - mid.astype(F32)).astype(BF16)
    dot = functools.partial(jnp.dot, preferred_element_type=F32)
    return dot(tri, hi) + dot(tri, mid) + dot(tri, lo)


def _mixout_kernel(proj_ref, gates_ref, bias_ref, convw_ref, mnw_ref,
                   c0_ref, n0_ref, m0_ref, t0_ref, h_ref, wo_ref, nwf_ref,
                   hnew_ref, hf_ref, cn_ref, nn_ref, mn_ref, tn_ref,
                   c_scr, n_scr, m_scr, a_scr, y_prev, y_cur,
                   *, n_valid, n_chunks, n_steps):
    c = CHUNK
    s = pl.program_id(0)
    chunk = jnp.minimum(s, n_steps - 1)

    @pl.when(s == 0)
    def _():
        y_prev[...] = jnp.zeros_like(y_prev)

    @pl.when(jnp.logical_and(lax.rem(chunk, n_chunks) == 0, s < n_steps))
    def _():
        c_scr[...] = c0_ref[...]
        n_scr[...] = n0_ref[...]
        m_scr[...] = m0_ref[...]
        a_scr[0:SUBLANES, :] = t0_ref[...]

    graw = gates_ref[...]
    g = graw + pltpu.roll(graw, LANES - N_GATES, axis=1) + bias_ref[...]
    capped = GATE_CAP * jnp.tanh(g / GATE_CAP)
    lane = lax.broadcasted_iota(jnp.int32, (c, LANES), 1)
    log_i = capped
    log_f = _log_sigmoid(capped)
    if n_valid < c:
        row_ok = lax.broadcasted_iota(jnp.int32, (c, LANES), 0) < n_valid
        log_i = jnp.where(row_ok, log_i, -jnp.inf)
        log_f = jnp.where(row_ok, log_f, 0.0)
    row = lax.broadcasted_iota(jnp.int32, (c, c), 0)
    col = lax.broadcasted_iota(jnp.int32, (c, c), 1)
    causal = row >= col
    tri = jnp.where(causal, 1.0, 0.0).astype(BF16)
    bcum = _cumsum_rows(jnp.where(lane >= HEADS, log_f, 0.0), tri)
    x_c = jnp.where(lane < HEADS, log_i, bcum)
    x_r = x_c.T

    def out_block(idx):
        n0 = idx * OUT_BLOCK
        return jnp.dot(y_prev[...], wo_ref[:, n0:n0 + OUT_BLOCK],
                       preferred_element_type=F32)

    out_parts = []
    for hd in range(HEADS):
        out_parts.append(out_block(hd))
        q = proj_ref[:, OFF_Q + hd * DQK:OFF_Q + (hd + 1) * DQK]
        k = proj_ref[:, OFF_K + hd * DQK:OFF_K + (hd + 1) * DQK]
        v = proj_ref[:, OFF_V + hd * DV:OFF_V + (hd + 1) * DV]
        og = proj_ref[:, OFF_OG + hd * DV:OFF_OG + (hd + 1) * DV].astype(F32)
        li_c = x_c[:, hd:hd + 1]
        b_c = x_c[:, HEADS + hd:HEADS + hd + 1]
        li_r = x_r[hd:hd + 1, :]
        b_r = x_r[HEADS + hd:HEADS + hd + 1, :]
        c_st = c_scr[hd]
        n_st = n_scr[hd]
        m_st = m_scr[hd][:, 0:1]

        dmat = jnp.where(causal, b_c - b_r + li_r, -jnp.inf)
        inter = b_c + m_st
        m_t = jnp.maximum(inter, jnp.max(dmat, axis=-1, keepdims=True))
        w_inter = jnp.exp(inter - m_t)
        sc = lax.dot_general(q, k, (((1,), (1,)), ((), ())),
                             preferred_element_type=F32)
        s_w = sc * jnp.exp(dmat - (m_t - LOG_QK_SCALE))
        qc = jnp.dot(q, c_st.astype(BF16), preferred_element_type=F32)
        num = (w_inter * QK_SCALE) * qc + jnp.dot(s_w.astype(BF16), v,
                                                   preferred_element_type=F32)
        qn = jnp.sum(q.astype(F32) * n_st, axis=-1, keepdims=True) * QK_SCALE
        den = w_inter * qn + jnp.sum(s_w, axis=-1, keepdims=True)
        hval = num / jnp.maximum(jnp.abs(den), jnp.exp(-m_t))

        b_end = b_c[c - 1:c, :]
        decay = b_end - b_c + li_c
        m_new = jnp.maximum(b_end + m_st, jnp.max(decay, axis=0, keepdims=True))
        w_old = jnp.exp(b_end + m_st - m_new)
        kw = k.astype(F32) * jnp.exp(decay - m_new)
        c_scr[hd] = w_old * c_st + jnp.dot(kw.T.astype(BF16), v,
                                           preferred_element_type=F32)
        n_scr[hd] = w_old * n_st + jnp.sum(kw, axis=0, keepdims=True)
        m_scr[hd] = jnp.broadcast_to(m_new, (1, LANES))

        hm = _rms_scale(hval, mnw_ref[:, hd * DV:(hd + 1) * DV])
        y_cur[:, hd * DV:(hd + 1) * DV] = (jax.nn.sigmoid(og) * hm).astype(BF16)

    u = proj_ref[:, OFF_U:OFF_U + CONV_W].astype(F32)
    gb = proj_ref[:, OFF_GB:OFF_GB + CONV_W].astype(F32)
    gc = proj_ref[:, OFF_GC:OFF_GC + CONV_W].astype(F32)
    a = gc * u
    a_scr[SUBLANES:SUBLANES + c, :] = a
    a1 = a_scr[SUBLANES - 1:SUBLANES - 1 + c, :]
    a2 = a_scr[SUBLANES - 2:SUBLANES - 2 + c, :]
    conv = a2 * convw_ref[0:1, :] + a1 * convw_ref[1:2, :] + a * convw_ref[2:3, :]
    y_cur[:, MLSTM_W:MLSTM_W + CONV_W] = (gb * conv).astype(BF16)
    a_scr[0:SUBLANES, :] = a_scr[n_valid:n_valid + SUBLANES, :]

    hnew = h_ref[...] + jnp.concatenate(out_parts, axis=1)
    hnew_ref[...] = hnew
    hf_ref[...] = _rms_scale(hnew, nwf_ref[...]).astype(BF16)

    y_prev[...] = y_cur[...]

    @pl.when(s == n_steps - 1)
    def _():
        cn_ref[...] = c_scr[...]
        nn_ref[...] = n_scr[...]
        mn_ref[...] = m_scr[...]
        tn_ref[...] = a_scr[0:SUBLANES, :]


def _mixout(proj, gates, bias, convw, mnw, state, h, wo, nwf, layer, *, n_seq, n_valid):
    m = proj.shape[0]
    n_steps = m // CHUNK
    n_chunks = n_steps // n_seq
    mix_rows = lambda s: (jnp.minimum(s, n_steps - 1), 0)
    out_rows = lambda s: (jnp.maximum(s - 1, 0), 0)
    const2 = lambda s: (0, 0)
    const3 = lambda s: (0, 0, 0)
    per_layer = lambda s: (layer, 0, 0)
    state_specs = [
        pl.BlockSpec((HEADS, DQK, DV), const3),
        pl.BlockSpec((HEADS, 1, LANES), const3),
        pl.BlockSpec((HEADS, 1, LANES), const3),
        pl.BlockSpec((SUBLANES, CONV_W), const2),
    ]
    state_shapes = [
        jax.ShapeDtypeStruct((HEADS, DQK, DV), F32),
        jax.ShapeDtypeStruct((HEADS, 1, LANES), F32),
        jax.ShapeDtypeStruct((HEADS, 1, LANES), F32),
        jax.ShapeDtypeStruct((SUBLANES, CONV_W), F32),
    ]
    out = pl.pallas_call(
        functools.partial(_mixout_kernel, n_valid=n_valid, n_chunks=n_chunks,
                          n_steps=n_steps),
        grid=(n_steps + 1,),
        in_specs=[
            pl.BlockSpec((CHUNK, PROJ_W), mix_rows),
            pl.BlockSpec((CHUNK, LANES), mix_rows),
            pl.BlockSpec((None, 1, LANES), per_layer),
            pl.BlockSpec((None, SUBLANES, CONV_W), per_layer),
            pl.BlockSpec((None, 1, MLSTM_W), per_layer),
        ] + state_specs + [
            pl.BlockSpec((CHUNK, D_MODEL), out_rows),
            pl.BlockSpec((None, D_MODEL, D_MODEL), per_layer),
            pl.BlockSpec((None, 1, D_MODEL), per_layer),
        ],
        out_specs=[pl.BlockSpec((CHUNK, D_MODEL), out_rows),
                   pl.BlockSpec((CHUNK, D_MODEL), out_rows)] + state_specs,
        out_shape=[jax.ShapeDtypeStruct((m, D_MODEL), F32),
                   jax.ShapeDtypeStruct((m, D_MODEL), BF16)] + state_shapes,
        scratch_shapes=[
            pltpu.VMEM((HEADS, DQK, DV), F32),
            pltpu.VMEM((HEADS, 1, LANES), F32),
            pltpu.VMEM((HEADS, 1, LANES), F32),
            pltpu.VMEM((SUBLANES + CHUNK, CONV_W), F32),
            pltpu.VMEM((CHUNK, D_MODEL), BF16),
            pltpu.VMEM((CHUNK, D_MODEL), BF16),
        ],
        compiler_params=_params("arbitrary"),
        name="mixout",
    )(proj, gates, bias, convw, mnw, *state, h, wo, nwf)
    return out[0], out[1], tuple(out[2:])


def _ffn_kernel(*refs, tm, norm_out, cast_next):
    if cast_next:
        (hf_ref, h_hbm, wg_ref, wu_ref, wd_ref, nw_ref, ng_ref, nu_ref, nd_ref,
         out_ref, cg_ref, cu_ref, cd_ref, hbuf, sem) = refs
    else:
        hf_ref, h_hbm, wg_ref, wu_ref, wd_ref, nw_ref, out_ref, hbuf, sem = refs
    i = pl.program_id(0)
    k = pl.program_id(1)

    def residual_copy(tile):
        return pltpu.make_async_copy(h_hbm.at[pl.ds(tile * tm, tm), :], hbuf, sem)

    def contribution():
        if cast_next:
            cg_ref[...] = ng_ref[...].astype(BF16)
            cu_ref[...] = nu_ref[...].astype(BF16)
            cd_ref[...] = nd_ref[...].astype(BF16)
        x = hf_ref[...]
        g = jnp.dot(x, wg_ref[...], preferred_element_type=F32)
        u = jnp.dot(x, wu_ref[...], preferred_element_type=F32)
        act = (g * jax.nn.sigmoid(g) * u).astype(BF16)
        return jnp.dot(act, wd_ref[...], preferred_element_type=F32)

    @pl.when(k == 0)
    def _():
        @pl.when(i == 0)
        def _():
            residual_copy(0).start()

        residual_copy(i).wait()
        out_ref[...] = hbuf[...] + contribution()

        @pl.when(i + 1 < pl.num_programs(0))
        def _():
            residual_copy(i + 1).start()

    @pl.when(k > 0)
    def _():
        out_ref[...] += contribution()

    if norm_out:
        @pl.when(k == pl.num_programs(1) - 1)
        def _():
            out_ref[...] = _rms_scale(out_ref[...], nw_ref[...])


def _ffn(hf, h, weights, layer, nw_out, *, tm, tf, norm_out, cast_next=None):
    wg, wu, wd = weights
    m = hf.shape[0]
    d_ff = wg.shape[2]
    n_i, n_k = m // tm, d_ff // tf
    rows = lambda i, k: (i, 0)
    in_specs = [
        pl.BlockSpec((tm, D_MODEL), rows),
        pl.BlockSpec(memory_space=pl.ANY),
        pl.BlockSpec((None, D_MODEL, tf), lambda i, k: (layer, 0, k)),
        pl.BlockSpec((None, D_MODEL, tf), lambda i, k: (layer, 0, k)),
        pl.BlockSpec((None, tf, D_MODEL), lambda i, k: (layer, k, 0)),
        pl.BlockSpec((1, D_MODEL), lambda i, k: (0, 0)),
    ]
    out_specs = [pl.BlockSpec((tm, D_MODEL), rows)]
    out_shape = [jax.ShapeDtypeStruct((m, D_MODEL), F32)]
    operands = [hf, h, wg, wu, wd, nw_out]
    if cast_next:
        ng, nu, nd, nxt = cast_next
        rb = D_MODEL // n_i
        assert rb * n_i == D_MODEL and rb % LANES == 0
        in_specs += [
            pl.BlockSpec((None, rb, tf), lambda i, k: (nxt, i, k)),
            pl.BlockSpec((None, rb, tf), lambda i, k: (nxt, i, k)),
            pl.BlockSpec((None, tf, rb), lambda i, k: (nxt, k, i)),
        ]
        out_specs += [
            pl.BlockSpec((None, rb, tf), lambda i, k: (0, i, k)),
            pl.BlockSpec((None, rb, tf), lambda i, k: (0, i, k)),
            pl.BlockSpec((None, tf, rb), lambda i, k: (0, k, i)),
        ]
        out_shape += [
            jax.ShapeDtypeStruct((1, D_MODEL, d_ff), BF16),
            jax.ShapeDtypeStruct((1, D_MODEL, d_ff), BF16),
            jax.ShapeDtypeStruct((1, d_ff, D_MODEL), BF16),
        ]
        operands += [ng, nu, nd]
    out = pl.pallas_call(
        functools.partial(_ffn_kernel, tm=tm, norm_out=norm_out,
                          cast_next=bool(cast_next)),
        grid=(n_i, n_k),
        in_specs=in_specs,
        out_specs=out_specs,
        out_shape=out_shape,
        scratch_shapes=[pltpu.VMEM((tm, D_MODEL), F32),
                        pltpu.SemaphoreType.DMA(())],
        compiler_params=_params("arbitrary", "arbitrary"),
        name="ffn",
    )(*operands)
    return out[0], tuple(out[1:])


def kernel(x, meta_tokens, norm_mix_w, w_in, b_gates, conv_w, mlstm_norm_w, w_out,
           norm_ffn_w, w_gate, w_up, w_down, norm_final_w):
    bsz, seq, _ = x.shape
    depth = w_in.shape[0]
    assert seq % CHUNK == 0 and N_META <= META_ROWS

    h_x = x.reshape(bsz * seq, D_MODEL)
    h_m = jnp.concatenate(
        [meta_tokens.astype(x.dtype), jnp.zeros((META_ROWS - N_META, D_MODEL), x.dtype)])

    w_main, w_gates = _pack_win(jnp.swapaxes(w_in, 1, 2), tb=512)
    wo = _cast_bf16(w_out, rb=512)
    ffn_w = (_cast_bf16(w_gate, rb=256, n_layers=1),
             _cast_bf16(w_up, rb=256, n_layers=1),
             _cast_bf16(w_down, rb=512, n_layers=1))
    bias = jnp.pad(b_gates, ((0, 0), (0, LANES - N_GATES)))[:, None, :]
    convw = jnp.pad(conv_w, ((0, 0), (0, SUBLANES - CONV_K), (0, 0)))
    nw_mix = norm_mix_w[:, None, :]
    nw_ffn = norm_ffn_w[:, None, :]
    mnw = mlstm_norm_w[:, None, :]
    nw_final = norm_final_w[None, :]

    zero_state = (jnp.zeros((HEADS, DQK, DV), F32),
                  jnp.zeros((HEADS, 1, LANES), F32),
                  jnp.zeros((HEADS, 1, LANES), F32),
                  jnp.zeros((SUBLANES, CONV_W), F32))

    for l in range(depth):
        last = l == depth - 1
        proj, gates = _inproj(h_m, nw_mix, w_main, w_gates, l, tm=META_ROWS, tn=1024)
        h_m, hf, state = _mixout(proj, gates, bias, convw, mnw, zero_state,
                                 h_m, wo, nw_ffn, l, n_seq=1, n_valid=N_META)
        h_m, _ = _ffn(hf, h_m, ffn_w, 0, nw_final, tm=META_ROWS, tf=512, norm_out=False)

        proj, gates = _inproj(h_x, nw_mix, w_main, w_gates, l, tm=1024, tn=2048)
        h_x, hf, _ = _mixout(proj, gates, bias, convw, mnw, state,
                             h_x, wo, nw_ffn, l, n_seq=bsz, n_valid=CHUNK)
        cast_next = None if last else (w_gate, w_up, w_down, l + 1)
        h_x, next_w = _ffn(hf, h_x, ffn_w, 0, nw_final, tm=1024, tf=512,
                           norm_out=last, cast_next=cast_next)
        ffn_w = next_w if cast_next else ffn_w

    return h_x.reshape(bsz, seq, D_MODEL)
```

```python
import functools
import math

import jax
import jax.numpy as jnp
from jax import lax
from jax.experimental import pallas as pl
from jax.experimental.pallas import tpu as pltpu

D_MODEL = 2048
N_META = 16
MLSTM_W = D_MODEL // 2
CONV_W = D_MODEL - MLSTM_W
HEADS = 4
DV = MLSTM_W // HEADS
DQK = DV // 2
QK_W = HEADS * DQK
CONV_K = 3
GATE_CAP = 15.0
EPS = 1e-6
QK_SCALE = DQK ** -0.5
LOG_QK_SCALE = math.log(QK_SCALE)

OFF_Q = 0
OFF_K = QK_W
OFF_V = 2 * QK_W
OFF_OG = OFF_V + MLSTM_W
OFF_U = OFF_OG + MLSTM_W
OFF_GB = OFF_U + CONV_W
OFF_GC = OFF_GB + CONV_W
PROJ_W = OFF_GC + CONV_W
GATE_COL0 = OFF_OG + MLSTM_W
N_GATES = 2 * HEADS
D_IN = PROJ_W + N_GATES

LANES = 128
SUBLANES = 8
CHUNK = 256
OUT_BLOCK = D_MODEL // HEADS
META_ROWS = CHUNK
VMEM_LIMIT = 56 * 1024 * 1024

F32 = jnp.float32
BF16 = jnp.bfloat16


def _params(*semantics):
    return pltpu.CompilerParams(dimension_semantics=semantics,
                                vmem_limit_bytes=VMEM_LIMIT)


def _rms_scale(x, w):
    ms = jnp.mean(x * x, axis=-1, keepdims=True)
    return x * lax.rsqrt(ms + EPS) * w


def _cast_kernel(x_ref, o_ref):
    o_ref[...] = x_ref[...].astype(BF16)


def _cast_bf16(w, *, rb, n_layers=None):
    depth, rows, cols = w.shape
    n_layers = depth if n_layers is None else n_layers
    spec = pl.BlockSpec((None, rb, cols), lambda l, i: (l, i, 0))
    return pl.pallas_call(
        _cast_kernel,
        grid=(n_layers, rows // rb),
        in_specs=[spec],
        out_specs=spec,
        out_shape=jax.ShapeDtypeStruct((n_layers, rows, cols), BF16),
        compiler_params=_params("arbitrary", "arbitrary"),
        name="cast_bf16",
    )(w)


def _pack_win_kernel(a_ref, b_ref, main_ref, gate_ref, *, tb):
    j = pl.program_id(1)
    first_shifted = GATE_COL0 // tb

    @pl.when(j < first_shifted)
    def _():
        main_ref[...] = a_ref[...].T.astype(BF16)

    @pl.when(j >= first_shifted)
    def _():
        x = jnp.concatenate([a_ref[N_GATES:, :], b_ref[...]], axis=0)
        main_ref[...] = x.T.astype(BF16)

    @pl.when(j == first_shifted)
    def _():
        w = a_ref[0:LANES, :].T
        hi = w.astype(BF16).astype(F32)
        lo = pltpu.roll(w - hi, N_GATES, axis=1)
        lane = lax.broadcasted_iota(jnp.int32, w.shape, 1)
        gate_ref[...] = jnp.where(lane < N_GATES, hi,
                                  jnp.where(lane < 2 * N_GATES, lo, 0.0)).astype(BF16)


def _pack_win(w_in_t, *, tb):
    depth = w_in_t.shape[0]
    assert GATE_COL0 % tb == 0 and PROJ_W % tb == 0 and N_GATES == SUBLANES
    return pl.pallas_call(
        functools.partial(_pack_win_kernel, tb=tb),
        grid=(depth, PROJ_W // tb),
        in_specs=[
            pl.BlockSpec((None, tb, D_MODEL), lambda l, j: (l, j, 0)),
            pl.BlockSpec((None, N_GATES, D_MODEL),
                         lambda l, j: (l, (j + 1) * (tb // N_GATES), 0)),
        ],
        out_specs=[pl.BlockSpec((None, D_MODEL, tb), lambda l, j: (l, 0, j)),
                   pl.BlockSpec((None, D_MODEL, LANES), lambda l, j: (l, 0, 0))],
        out_shape=[jax.ShapeDtypeStruct((depth, D_MODEL, PROJ_W), BF16),
                   jax.ShapeDtypeStruct((depth, D_MODEL, LANES), BF16)],
        compiler_params=_params("arbitrary", "arbitrary"),
        name="pack_win",
    )(w_in_t, w_in_t)


def _inproj_kernel(h_hbm, nw_ref, w_ref, wg_ref, proj_ref, gates_ref, hn_ref, hbuf, sem,
                   *, tm):
    i = pl.program_id(0)

    def h_copy(tile):
        return pltpu.make_async_copy(h_hbm.at[pl.ds(tile * tm, tm), :], hbuf, sem)

    @pl.when(pl.program_id(1) == 0)
    def _():
        @pl.when(i == 0)
        def _():
            h_copy(0).start()

        h_copy(i).wait()
        hn = _rms_scale(hbuf[...], nw_ref[...]).astype(BF16)
        hn_ref[...] = hn
        gates_ref[...] = jnp.dot(hn, wg_ref[...], preferred_element_type=F32)

        @pl.when(i + 1 < pl.num_programs(0))
        def _():
            h_copy(i + 1).start()

    proj_ref[...] = jnp.dot(hn_ref[...], w_ref[...],
                            preferred_element_type=F32).astype(BF16)


def _inproj(h, nw, w, wg, layer, *, tm, tn):
    m = h.shape[0]
    return pl.pallas_call(
        functools.partial(_inproj_kernel, tm=tm),
        grid=(m // tm, PROJ_W // tn),
        in_specs=[
            pl.BlockSpec(memory_space=pl.ANY),
            pl.BlockSpec((None, 1, D_MODEL), lambda i, j: (layer, 0, 0)),
            pl.BlockSpec((None, D_MODEL, tn), lambda i, j: (layer, 0, j)),
            pl.BlockSpec((None, D_MODEL, LANES), lambda i, j: (layer, 0, 0)),
        ],
        out_specs=[
            pl.BlockSpec((tm, tn), lambda i, j: (i, j)),
            pl.BlockSpec((tm, LANES), lambda i, j: (i, 0)),
        ],
        out_shape=[
            jax.ShapeDtypeStruct((m, PROJ_W), BF16),
            jax.ShapeDtypeStruct((m, LANES), F32),
        ],
        scratch_shapes=[pltpu.VMEM((tm, D_MODEL), BF16),
                        pltpu.VMEM((tm, D_MODEL), F32),
                        pltpu.SemaphoreType.DMA(())],
        compiler_params=_params("arbitrary", "arbitrary"),
        name="inproj",
    )(h, nw, w, wg)


def _log_sigmoid(x):
    return jnp.minimum(x, 0.0) - jnp.log1p(jnp.exp(-jnp.abs(x)))


def _cumsum_rows(x, tri):
    hi = x.astype(BF16)
    r1 = x - hi.astype(F32)
    mid = r1.astype(BF16)
    lo = (r1 - mid.astype(F32)).astype(BF16)
    dot = functools.partial(jnp.dot, preferred_element_type=F32)
    return dot(tri, hi) + dot(tri, mid) + dot(tri, lo)


def _mixout_kernel(proj_ref, gates_ref, bias_ref, convw_ref, mnw_ref,
                   c0_ref, n0_ref, m0_ref, t0_ref, h_ref, wo_ref, nwf_ref,
                   hnew_ref, hf_ref, cn_ref, nn_ref, mn_ref, tn_ref,
                   c_scr, n_scr, m_scr, a_scr, y_prev, y_cur,
                   *, n_valid, n_chunks, n_steps):
    c = CHUNK
    s = pl.program_id(0)
    chunk = jnp.minimum(s, n_steps - 1)

    @pl.when(s == 0)
    def _():
        y_prev[...] = jnp.zeros_like(y_prev)

    @pl.when(jnp.logical_and(lax.rem(chunk, n_chunks) == 0, s < n_steps))
    def _():
        c_scr[...] = c0_ref[...]
        n_scr[...] = n0_ref[...]
        m_scr[...] = m0_ref[...]
        a_scr[0:SUBLANES, :] = t0_ref[...]

    graw = gates_ref[...]
    g = graw + pltpu.roll(graw, LANES - N_GATES, axis=1) + bias_ref[...]
    capped = GATE_CAP * jnp.tanh(g / GATE_CAP)
    lane = lax.broadcasted_iota(jnp.int32, (c, LANES), 1)
    log_i = capped
    log_f = _log_sigmoid(capped)
    if n_valid < c:
        row_ok = lax.broadcasted_iota(jnp.int32, (c, LANES), 0) < n_valid
        log_i = jnp.where(row_ok, log_i, -jnp.inf)
        log_f = jnp.where(row_ok, log_f, 0.0)
    row = lax.broadcasted_iota(jnp.int32, (c, c), 0)
    col = lax.broadcasted_iota(jnp.int32, (c, c), 1)
    causal = row >= col
    tri = jnp.where(causal, 1.0, 0.0).astype(BF16)
    bcum = _cumsum_rows(jnp.where(lane >= HEADS, log_f, 0.0), tri)
    x_c = jnp.where(lane < HEADS, log_i, bcum)
    x_r = x_c.T

    def out_block(idx):
        n0 = idx * OUT_BLOCK
        return jnp.dot(y_prev[...], wo_ref[:, n0:n0 + OUT_BLOCK],
                       preferred_element_type=F32)

    out_parts = []
    for hd in range(HEADS):
        out_parts.append(out_block(hd))
        q = proj_ref[:, OFF_Q + hd * DQK:OFF_Q + (hd + 1) * DQK]
        k = proj_ref[:, OFF_K + hd * DQK:OFF_K + (hd + 1) * DQK]
        v = proj_ref[:, OFF_V + hd * DV:OFF_V + (hd + 1) * DV]
        og = proj_ref[:, OFF_OG + hd * DV:OFF_OG + (hd + 1) * DV].astype(F32)
        li_c = x_c[:, hd:hd + 1]
        b_c = x_c[:, HEADS + hd:HEADS + hd + 1]
        li_r = x_r[hd:hd + 1, :]
        b_r = x_r[HEADS + hd:HEADS + hd + 1, :]
        c_st = c_scr[hd]
        n_st = n_scr[hd]
        m_st = m_scr[hd][:, 0:1]

        dmat = jnp.where(causal, b_c - b_r + li_r, -jnp.inf)
        inter = b_c + m_st
        m_t = jnp.maximum(inter, jnp.max(dmat, axis=-1, keepdims=True))
        w_inter = jnp.exp(inter - m_t)
        sc = lax.dot_general(q, k, (((1,), (1,)), ((), ())),
                             preferred_element_type=F32)
        s_w = sc * jnp.exp(dmat - (m_t - LOG_QK_SCALE))
        qc = jnp.dot(q, c_st.astype(BF16), preferred_element_type=F32)
        num = (w_inter * QK_SCALE) * qc + jnp.dot(s_w.astype(BF16), v,
                                                   preferred_element_type=F32)
        qn = jnp.sum(q.astype(F32) * n_st, axis=-1, keepdims=True) * QK_SCALE
        den = w_inter * qn + jnp.sum(s_w, axis=-1, keepdims=True)
        hval = num / jnp.maximum(jnp.abs(den), jnp.exp(-m_t))

        b_end = b_c[c - 1:c, :]
        decay = b_end - b_c + li_c
        m_new = jnp.maximum(b_end + m_st, jnp.max(decay, axis=0, keepdims=True))
        w_old = jnp.exp(b_end + m_st - m_new)
        kw = k.astype(F32) * jnp.exp(decay - m_new)
        c_scr[hd] = w_old * c_st + jnp.dot(kw.T.astype(BF16), v,
                                           preferred_element_type=F32)
        n_scr[hd] = w_old * n_st + jnp.sum(kw, axis=0, keepdims=True)
        m_scr[hd] = jnp.broadcast_to(m_new, (1, LANES))

        hm = _rms_scale(hval, mnw_ref[:, hd * DV:(hd + 1) * DV])
        y_cur[:, hd * DV:(hd + 1) * DV] = (jax.nn.sigmoid(og) * hm).astype(BF16)

    u = proj_ref[:, OFF_U:OFF_U + CONV_W].astype(F32)
    gb = proj_ref[:, OFF_GB:OFF_GB + CONV_W].astype(F32)
    gc = proj_ref[:, OFF_GC:OFF_GC + CONV_W].astype(F32)
    a = gc * u
    a_scr[SUBLANES:SUBLANES + c, :] = a
    a1 = a_scr[SUBLANES - 1:SUBLANES - 1 + c, :]
    a2 = a_scr[SUBLANES - 2:SUBLANES - 2 + c, :]
    conv = a2 * convw_ref[0:1, :] + a1 * convw_ref[1:2, :] + a * convw_ref[2:3, :]
    y_cur[:, MLSTM_W:MLSTM_W + CONV_W] = (gb * conv).astype(BF16)
    a_scr[0:SUBLANES, :] = a_scr[n_valid:n_valid + SUBLANES, :]

    hnew = h_ref[...] + jnp.concatenate(out_parts, axis=1)
    hnew_ref[...] = hnew
    hf_ref[...] = _rms_scale(hnew, nwf_ref[...]).astype(BF16)

    y_prev[...] = y_cur[...]

    @pl.when(s == n_steps - 1)
    def _():
        cn_ref[...] = c_scr[...]
        nn_ref[...] = n_scr[...]
        mn_ref[...] = m_scr[...]
        tn_ref[...] = a_scr[0:SUBLANES, :]


def _mixout(proj, gates, bias, convw, mnw, state, h, wo, nwf, layer, *, n_seq, n_valid):
    m = proj.shape[0]
    n_steps = m // CHUNK
    n_chunks = n_steps // n_seq
    mix_rows = lambda s: (jnp.minimum(s, n_steps - 1), 0)
    out_rows = lambda s: (jnp.maximum(s - 1, 0), 0)
    const2 = lambda s: (0, 0)
    const3 = lambda s: (0, 0, 0)
    per_layer = lambda s: (layer, 0, 0)
    state_specs = [
        pl.BlockSpec((HEADS, DQK, DV), const3),
        pl.BlockSpec((HEADS, 1, LANES), const3),
        pl.BlockSpec((HEADS, 1, LANES), const3),
        pl.BlockSpec((SUBLANES, CONV_W), const2),
    ]
    state_shapes = [
        jax.ShapeDtypeStruct((HEADS, DQK, DV), F32),
        jax.ShapeDtypeStruct((HEADS, 1, LANES), F32),
        jax.ShapeDtypeStruct((HEADS, 1, LANES), F32),
        jax.ShapeDtypeStruct((SUBLANES, CONV_W), F32),
    ]
    out = pl.pallas_call(
        functools.partial(_mixout_kernel, n_valid=n_valid, n_chunks=n_chunks,
                          n_steps=n_steps),
        grid=(n_steps + 1,),
        in_specs=[
            pl.BlockSpec((CHUNK, PROJ_W), mix_rows),
            pl.BlockSpec((CHUNK, LANES), mix_rows),
            pl.BlockSpec((None, 1, LANES), per_layer),
            pl.BlockSpec((None, SUBLANES, CONV_W), per_layer),
            pl.BlockSpec((None, 1, MLSTM_W), per_layer),
        ] + state_specs + [
            pl.BlockSpec((CHUNK, D_MODEL), out_rows),
            pl.BlockSpec((None, D_MODEL, D_MODEL), per_layer),
            pl.BlockSpec((None, 1, D_MODEL), per_layer),
        ],
        out_specs=[pl.BlockSpec((CHUNK, D_MODEL), out_rows),
                   pl.BlockSpec((CHUNK, D_MODEL), out_rows)] + state_specs,
        out_shape=[jax.ShapeDtypeStruct((m, D_MODEL), F32),
                   jax.ShapeDtypeStruct((m, D_MODEL), BF16)] + state_shapes,
        scratch_shapes=[
            pltpu.VMEM((HEADS, DQK, DV), F32),
            pltpu.VMEM((HEADS, 1, LANES), F32),
            pltpu.VMEM((HEADS, 1, LANES), F32),
            pltpu.VMEM((SUBLANES + CHUNK, CONV_W), F32),
            pltpu.VMEM((CHUNK, D_MODEL), BF16),
            pltpu.VMEM((CHUNK, D_MODEL), BF16),
        ],
        compiler_params=_params("arbitrary"),
        name="mixout",
    )(proj, gates, bias, convw, mnw, *state, h, wo, nwf)
    return out[0], out[1], tuple(out[2:])


def _ffn_kernel(*refs, tm, norm_out, cast):
    if cast == "next":
        (hf_ref, h_hbm, wg_ref, wu_ref, wd_ref, nw_ref, ng_ref, nu_ref, nd_ref,
         out_ref, cg_ref, cu_ref, cd_ref, hbuf, sem) = refs
    elif cast == "self":
        (hf_ref, h_hbm, wg_ref, wu_ref, wd_ref, nw_ref,
         out_ref, cg_ref, cu_ref, cd_ref, hbuf, sem) = refs
    else:
        hf_ref, h_hbm, wg_ref, wu_ref, wd_ref, nw_ref, out_ref, hbuf, sem = refs
    i = pl.program_id(0)
    k = pl.program_id(1)

    def residual_copy(tile):
        return pltpu.make_async_copy(h_hbm.at[pl.ds(tile * tm, tm), :], hbuf, sem)

    def contribution():
        wg, wu, wd = wg_ref[...], wu_ref[...], wd_ref[...]
        if cast == "next":
            cg_ref[...] = ng_ref[...].astype(BF16)
            cu_ref[...] = nu_ref[...].astype(BF16)
            cd_ref[...] = nd_ref[...].astype(BF16)
        elif cast == "self":
            wg, wu, wd = wg.astype(BF16), wu.astype(BF16), wd.astype(BF16)
            cg_ref[...] = wg
            cu_ref[...] = wu
            cd_ref[...] = wd
        x = hf_ref[...]
        g = jnp.dot(x, wg, preferred_element_type=F32)
        u = jnp.dot(x, wu, preferred_element_type=F32)
        act = (g * jax.nn.sigmoid(g) * u).astype(BF16)
        return jnp.dot(act, wd, preferred_element_type=F32)

    @pl.when(k == 0)
    def _():
        @pl.when(i == 0)
        def _():
            residual_copy(0).start()

        residual_copy(i).wait()
        out_ref[...] = hbuf[...] + contribution()

        @pl.when(i + 1 < pl.num_programs(0))
        def _():
            residual_copy(i + 1).start()

    @pl.when(k > 0)
    def _():
        out_ref[...] += contribution()

    if norm_out:
        @pl.when(k == pl.num_programs(1) - 1)
        def _():
            out_ref[...] = _rms_scale(out_ref[...], nw_ref[...])


def _ffn(hf, h, weights, layer, nw_out, *, tm, tf, norm_out, cast=None, cast_from=None):
    wg, wu, wd = weights
    m = hf.shape[0]
    d_ff = wg.shape[2]
    n_i, n_k = m // tm, d_ff // tf
    rows = lambda i, k: (i, 0)
    in_specs = [
        pl.BlockSpec((tm, D_MODEL), rows),
        pl.BlockSpec(memory_space=pl.ANY),
        pl.BlockSpec((None, D_MODEL, tf), lambda i, k: (layer, 0, k)),
        pl.BlockSpec((None, D_MODEL, tf), lambda i, k: (layer, 0, k)),
        pl.BlockSpec((None, tf, D_MODEL), lambda i, k: (layer, k, 0)),
        pl.BlockSpec((1, D_MODEL), lambda i, k: (0, 0)),
    ]
    out_specs = [pl.BlockSpec((tm, D_MODEL), rows)]
    out_shape = [jax.ShapeDtypeStruct((m, D_MODEL), F32)]
    operands = [hf, h, wg, wu, wd, nw_out]
    cast_shapes = [
        jax.ShapeDtypeStruct((1, D_MODEL, d_ff), BF16),
        jax.ShapeDtypeStruct((1, D_MODEL, d_ff), BF16),
        jax.ShapeDtypeStruct((1, d_ff, D_MODEL), BF16),
    ]
    if cast == "next":
        ng, nu, nd, nxt = cast_from
        rb = D_MODEL // n_i
        assert rb * n_i == D_MODEL and rb % LANES == 0
        in_specs += [
            pl.BlockSpec((None, rb, tf), lambda i, k: (nxt, i, k)),
            pl.BlockSpec((None, rb, tf), lambda i, k: (nxt, i, k)),
            pl.BlockSpec((None, tf, rb), lambda i, k: (nxt, k, i)),
        ]
        out_specs += [
            pl.BlockSpec((None, rb, tf), lambda i, k: (0, i, k)),
            pl.BlockSpec((None, rb, tf), lambda i, k: (0, i, k)),
            pl.BlockSpec((None, tf, rb), lambda i, k: (0, k, i)),
        ]
        out_shape += cast_shapes
        operands += [ng, nu, nd]
    elif cast == "self":
        assert n_i == 1
        out_specs += [
            pl.BlockSpec((None, D_MODEL, tf), lambda i, k: (0, 0, k)),
            pl.BlockSpec((None, D_MODEL, tf), lambda i, k: (0, 0, k)),
            pl.BlockSpec((None, tf, D_MODEL), lambda i, k: (0, k, 0)),
        ]
        out_shape += cast_shapes
    out = pl.pallas_call(
        functools.partial(_ffn_kernel, tm=tm, norm_out=norm_out, cast=cast),
        grid=(n_i, n_k),
        in_specs=in_specs,
        out_specs=out_specs,
        out_shape=out_shape,
        scratch_shapes=[pltpu.VMEM((tm, D_MODEL), F32),
                        pltpu.SemaphoreType.DMA(())],
        compiler_params=_params("arbitrary", "arbitrary"),
        name="ffn",
    )(*operands)
    return out[0], tuple(out[1:])


def kernel(x, meta_tokens, norm_mix_w, w_in, b_gates, conv_w, mlstm_norm_w, w_out,
           norm_ffn_w, w_gate, w_up, w_down, norm_final_w):
    bsz, seq, _ = x.shape
    depth = w_in.shape[0]
    assert seq % CHUNK == 0 and N_META <= META_ROWS

    h_x = x.reshape(bsz * seq, D_MODEL)
    h_m = jnp.concatenate(
        [meta_tokens.astype(x.dtype), jnp.zeros((META_ROWS - N_META, D_MODEL), x.dtype)])

    w_main, w_gates = _pack_win(jnp.swapaxes(w_in, 1, 2), tb=512)
    wo = _cast_bf16(w_out, rb=512)
    ffn_w = None
    bias = jnp.pad(b_gates, ((0, 0), (0, LANES - N_GATES)))[:, None, :]
    convw = jnp.pad(conv_w, ((0, 0), (0, SUBLANES - CONV_K), (0, 0)))
    nw_mix = norm_mix_w[:, None, :]
    nw_ffn = norm_ffn_w[:, None, :]
    mnw = mlstm_norm_w[:, None, :]
    nw_final = norm_final_w[None, :]

    zero_state = (jnp.zeros((HEADS, DQK, DV), F32),
                  jnp.zeros((HEADS, 1, LANES), F32),
                  jnp.zeros((HEADS, 1, LANES), F32),
                  jnp.zeros((SUBLANES, CONV_W), F32))

    for l in range(depth):
        last = l == depth - 1
        proj, gates = _inproj(h_m, nw_mix, w_main, w_gates, l, tm=META_ROWS, tn=1024)
        h_m, hf, state = _mixout(proj, gates, bias, convw, mnw, zero_state,
                                 h_m, wo, nw_ffn, l, n_seq=1, n_valid=N_META)
        if ffn_w is None:
            h_m, ffn_w = _ffn(hf, h_m, (w_gate, w_up, w_down), l, nw_final,
                              tm=META_ROWS, tf=512, norm_out=False, cast="self")
        else:
            h_m, _ = _ffn(hf, h_m, ffn_w, 0, nw_final, tm=META_ROWS, tf=512,
                          norm_out=False)

        proj, gates = _inproj(h_x, nw_mix, w_main, w_gates, l, tm=1024, tn=2048)
        h_x, hf, _ = _mixout(proj, gates, bias, convw, mnw, state,
                             h_x, wo, nw_ffn, l, n_seq=bsz, n_valid=CHUNK)
        if last:
            h_x, _ = _ffn(hf, h_x, ffn_w, 0, nw_final, tm=1024, tf=512, norm_out=True)
        else:
            h_x, ffn_w = _ffn(hf, h_x, ffn_w, 0, nw_final, tm=1024, tf=512,
                              norm_out=False, cast="next",
                              cast_from=(w_gate, w_up, w_down, l + 1))

    return h_x.reshape(bsz, seq, D_MODEL)
```

```python
import functools
import math

import jax
import jax.numpy as jnp
from jax import lax
from jax.experimental import pallas as pl
from jax.experimental.pallas import tpu as pltpu

D_MODEL = 2048
N_META = 16
MLSTM_W = D_MODEL // 2
CONV_W = D_MODEL - MLSTM_W
HEADS = 4
DV = MLSTM_W // HEADS
DQK = DV // 2
QK_W = HEADS * DQK
CONV_K = 3
GATE_CAP = 15.0
EPS = 1e-6
QK_SCALE = DQK ** -0.5
LOG_QK_SCALE = math.log(QK_SCALE)

OFF_Q = 0
OFF_K = QK_W
OFF_V = 2 * QK_W
OFF_OG = OFF_V + MLSTM_W
OFF_U = OFF_OG + MLSTM_W
OFF_GB = OFF_U + CONV_W
OFF_GC = OFF_GB + CONV_W
PROJ_W = OFF_GC + CONV_W
GATE_COL0 = OFF_OG + MLSTM_W
N_GATES = 2 * HEADS
D_IN = PROJ_W + N_GATES

LANES = 128
SUBLANES = 8
CHUNK = 256
OUT_BLOCK = D_MODEL // HEADS
RESIDUAL_PREFETCH_STEP = 2
META_ROWS = CHUNK
VMEM_LIMIT = 56 * 1024 * 1024

F32 = jnp.float32
BF16 = jnp.bfloat16


def _params(*semantics):
    return pltpu.CompilerParams(dimension_semantics=semantics,
                                vmem_limit_bytes=VMEM_LIMIT)


def _rms_scale(x, w):
    ms = jnp.mean(x * x, axis=-1, keepdims=True)
    return x * lax.rsqrt(ms + EPS) * w


def _cast_kernel(x_ref, o_ref):
    o_ref[...] = x_ref[...].astype(BF16)


def _cast_bf16(w, *, rb, n_layers=None):
    depth, rows, cols = w.shape
    n_layers = depth if n_layers is None else n_layers
    spec = pl.BlockSpec((None, rb, cols), lambda l, i: (l, i, 0))
    return pl.pallas_call(
        _cast_kernel,
        grid=(n_layers, rows // rb),
        in_specs=[spec],
        out_specs=spec,
        out_shape=jax.ShapeDtypeStruct((n_layers, rows, cols), BF16),
        compiler_params=_params("arbitrary", "arbitrary"),
        name="cast_bf16",
    )(w)


def _pack_win_kernel(a_ref, b_ref, main_ref, gate_ref, *, tb):
    j = pl.program_id(1)
    first_shifted = GATE_COL0 // tb

    @pl.when(j < first_shifted)
    def _():
        main_ref[...] = a_ref[...].T.astype(BF16)

    @pl.when(j >= first_shifted)
    def _():
        x = jnp.concatenate([a_ref[N_GATES:, :], b_ref[...]], axis=0)
        main_ref[...] = x.T.astype(BF16)

    @pl.when(j == first_shifted)
    def _():
        w = a_ref[0:LANES, :].T
        hi = w.astype(BF16).astype(F32)
        lo = pltpu.roll(w - hi, N_GATES, axis=1)
        lane = lax.broadcasted_iota(jnp.int32, w.shape, 1)
        gate_ref[...] = jnp.where(lane < N_GATES, hi,
                                  jnp.where(lane < 2 * N_GATES, lo, 0.0)).astype(BF16)


def _pack_win(w_in_t, *, tb):
    depth = w_in_t.shape[0]
    assert GATE_COL0 % tb == 0 and PROJ_W % tb == 0 and N_GATES == SUBLANES
    return pl.pallas_call(
        functools.partial(_pack_win_kernel, tb=tb),
        grid=(depth, PROJ_W // tb),
        in_specs=[
            pl.BlockSpec((None, tb, D_MODEL), lambda l, j: (l, j, 0)),
            pl.BlockSpec((None, N_GATES, D_MODEL),
                         lambda l, j: (l, (j + 1) * (tb // N_GATES), 0)),
        ],
        out_specs=[pl.BlockSpec((None, D_MODEL, tb), lambda l, j: (l, 0, j)),
                   pl.BlockSpec((None, D_MODEL, LANES), lambda l, j: (l, 0, 0))],
        out_shape=[jax.ShapeDtypeStruct((depth, D_MODEL, PROJ_W), BF16),
                   jax.ShapeDtypeStruct((depth, D_MODEL, LANES), BF16)],
        compiler_params=_params("arbitrary", "arbitrary"),
        name="pack_win",
    )(w_in_t, w_in_t)


def _inproj_kernel(h_hbm, nw_ref, w_ref, wg_ref, proj_ref, gates_ref, hn_ref, hbuf, sem,
                   *, tm):
    i = pl.program_id(0)

    def h_copy(tile):
        return pltpu.make_async_copy(h_hbm.at[pl.ds(tile * tm, tm), :], hbuf, sem)

    @pl.when(pl.program_id(1) == 0)
    def _():
        @pl.when(i == 0)
        def _():
            h_copy(0).start()

        h_copy(i).wait()
        hn = _rms_scale(hbuf[...], nw_ref[...]).astype(BF16)
        hn_ref[...] = hn
        gates_ref[...] = jnp.dot(hn, wg_ref[...], preferred_element_type=F32)

        @pl.when(i + 1 < pl.num_programs(0))
        def _():
            h_copy(i + 1).start()

    proj_ref[...] = jnp.dot(hn_ref[...], w_ref[...],
                            preferred_element_type=F32).astype(BF16)


def _inproj(h, nw, w, wg, layer, *, tm, tn):
    m = h.shape[0]
    return pl.pallas_call(
        functools.partial(_inproj_kernel, tm=tm),
        grid=(m // tm, PROJ_W // tn),
        in_specs=[
            pl.BlockSpec(memory_space=pl.ANY),
            pl.BlockSpec((None, 1, D_MODEL), lambda i, j: (layer, 0, 0)),
            pl.BlockSpec((None, D_MODEL, tn), lambda i, j: (layer, 0, j)),
            pl.BlockSpec((None, D_MODEL, LANES), lambda i, j: (layer, 0, 0)),
        ],
        out_specs=[
            pl.BlockSpec((tm, tn), lambda i, j: (i, j)),
            pl.BlockSpec((tm, LANES), lambda i, j: (i, 0)),
        ],
        out_shape=[
            jax.ShapeDtypeStruct((m, PROJ_W), BF16),
            jax.ShapeDtypeStruct((m, LANES), F32),
        ],
        scratch_shapes=[pltpu.VMEM((tm, D_MODEL), BF16),
                        pltpu.VMEM((tm, D_MODEL), F32),
                        pltpu.SemaphoreType.DMA(())],
        compiler_params=_params("arbitrary", "arbitrary"),
        name="inproj",
    )(h, nw, w, wg)


def _log_sigmoid(x):
    return jnp.minimum(x, 0.0) - jnp.log1p(jnp.exp(-jnp.abs(x)))


def _cumsum_rows(x, tri):
    hi = x.astype(BF16)
    r1 = x - hi.astype(F32)
    mid = r1.astype(BF16)
    lo = (r1 - mid.astype(F32)).astype(BF16)
    dot = functools.partial(jnp.dot, preferred_element_type=F32)
    return dot(tri, hi) + dot(tri, mid) + dot(tri, lo)


def _mixout_kernel(proj_ref, gates_ref, bias_ref, convw_ref, mnw_ref,
                   c0_ref, n0_ref, m0_ref, t0_ref, h_ref, wo_ref, nwf_ref,
                   hnew_ref, hf_ref, cn_ref, nn_ref, mn_ref, tn_ref,
                   c_scr, n_scr, m_scr, a_scr, y_prev, y_cur,
                   *, n_valid, n_chunks, n_steps):
    c = CHUNK
    s = pl.program_id(0)
    chunk = jnp.minimum(s, n_steps - 1)

    @pl.when(s == 0)
    def _():
        y_prev[...] = jnp.zeros_like(y_prev)

    @pl.when(jnp.logical_and(lax.rem(chunk, n_chunks) == 0, s < n_steps))
    def _():
        c_scr[...] = c0_ref[...]
        n_scr[...] = n0_ref[...]
        m_scr[...] = m0_ref[...]
        a_scr[0:SUBLANES, :] = t0_ref[...]

    graw = gates_ref[...]
    g = graw + pltpu.roll(graw, LANES - N_GATES, axis=1) + bias_ref[...]
    capped = GATE_CAP * jnp.tanh(g / GATE_CAP)
    lane = lax.broadcasted_iota(jnp.int32, (c, LANES), 1)
    log_i = capped
    log_f = _log_sigmoid(capped)
    if n_valid < c:
        row_ok = lax.broadcasted_iota(jnp.int32, (c, LANES), 0) < n_valid
        log_i = jnp.where(row_ok, log_i, -jnp.inf)
        log_f = jnp.where(row_ok, log_f, 0.0)
    row = lax.broadcasted_iota(jnp.int32, (c, c), 0)
    col = lax.broadcasted_iota(jnp.int32, (c, c), 1)
    causal = row >= col
    tri = jnp.where(causal, 1.0, 0.0).astype(BF16)
    bcum = _cumsum_rows(jnp.where(lane >= HEADS, log_f, 0.0), tri)
    x_c = jnp.where(lane < HEADS, log_i, bcum)
    x_r = x_c.T

    def out_block(idx):
        n0 = idx * OUT_BLOCK
        return jnp.dot(y_prev[...], wo_ref[:, n0:n0 + OUT_BLOCK],
                       preferred_element_type=F32)

    out_parts = []
    for hd in range(HEADS):
        out_parts.append(out_block(hd))
        q = proj_ref[:, OFF_Q + hd * DQK:OFF_Q + (hd + 1) * DQK]
        k = proj_ref[:, OFF_K + hd * DQK:OFF_K + (hd + 1) * DQK]
        v = proj_ref[:, OFF_V + hd * DV:OFF_V + (hd + 1) * DV]
        og = proj_ref[:, OFF_OG + hd * DV:OFF_OG + (hd + 1) * DV].astype(F32)
        li_c = x_c[:, hd:hd + 1]
        b_c = x_c[:, HEADS + hd:HEADS + hd + 1]
        li_r = x_r[hd:hd + 1, :]
        b_r = x_r[HEADS + hd:HEADS + hd + 1, :]
        c_st = c_scr[hd]
        n_st = n_scr[hd]
        m_st = m_scr[hd][:, 0:1]

        dmat = jnp.where(causal, b_c - b_r + li_r, -jnp.inf)
        inter = b_c + m_st
        m_t = jnp.maximum(inter, jnp.max(dmat, axis=-1, keepdims=True))
        w_inter = jnp.exp(inter - m_t)
        sc = lax.dot_general(q, k, (((1,), (1,)), ((), ())),
                             preferred_element_type=F32)
        s_w = sc * jnp.exp(dmat - (m_t - LOG_QK_SCALE))
        qc = jnp.dot(q, c_st.astype(BF16), preferred_element_type=F32)
        num = (w_inter * QK_SCALE) * qc + jnp.dot(s_w.astype(BF16), v,
                                                   preferred_element_type=F32)
        qn = jnp.sum(q.astype(F32) * n_st, axis=-1, keepdims=True) * QK_SCALE
        den = w_inter * qn + jnp.sum(s_w, axis=-1, keepdims=True)
        hval = num / jnp.maximum(jnp.abs(den), jnp.exp(-m_t))

        b_end = b_c[c - 1:c, :]
        decay = b_end - b_c + li_c
        m_new = jnp.maximum(b_end + m_st, jnp.max(decay, axis=0, keepdims=True))
        w_old = jnp.exp(b_end + m_st - m_new)
        kw = k.astype(F32) * jnp.exp(decay - m_new)
        c_scr[hd] = w_old * c_st + jnp.dot(kw.T.astype(BF16), v,
                                           preferred_element_type=F32)
        n_scr[hd] = w_old * n_st + jnp.sum(kw, axis=0, keepdims=True)
        m_scr[hd] = jnp.broadcast_to(m_new, (1, LANES))

        hm = _rms_scale(hval, mnw_ref[:, hd * DV:(hd + 1) * DV])
        y_cur[:, hd * DV:(hd + 1) * DV] = (jax.nn.sigmoid(og) * hm).astype(BF16)

    u = proj_ref[:, OFF_U:OFF_U + CONV_W].astype(F32)
    gb = proj_ref[:, OFF_GB:OFF_GB + CONV_W].astype(F32)
    gc = proj_ref[:, OFF_GC:OFF_GC + CONV_W].astype(F32)
    a = gc * u
    a_scr[SUBLANES:SUBLANES + c, :] = a
    a1 = a_scr[SUBLANES - 1:SUBLANES - 1 + c, :]
    a2 = a_scr[SUBLANES - 2:SUBLANES - 2 + c, :]
    conv = a2 * convw_ref[0:1, :] + a1 * convw_ref[1:2, :] + a * convw_ref[2:3, :]
    y_cur[:, MLSTM_W:MLSTM_W + CONV_W] = (gb * conv).astype(BF16)
    a_scr[0:SUBLANES, :] = a_scr[n_valid:n_valid + SUBLANES, :]

    hnew = h_ref[...] + jnp.concatenate(out_parts, axis=1)
    hnew_ref[...] = hnew
    hf_ref[...] = _rms_scale(hnew, nwf_ref[...]).astype(BF16)

    y_prev[...] = y_cur[...]

    @pl.when(s == n_steps - 1)
    def _():
        cn_ref[...] = c_scr[...]
        nn_ref[...] = n_scr[...]
        mn_ref[...] = m_scr[...]
        tn_ref[...] = a_scr[0:SUBLANES, :]


def _mixout(proj, gates, bias, convw, mnw, state, h, wo, nwf, layer, *, n_seq, n_valid):
    m = proj.shape[0]
    n_steps = m // CHUNK
    n_chunks = n_steps // n_seq
    mix_rows = lambda s: (jnp.minimum(s, n_steps - 1), 0)
    out_rows = lambda s: (jnp.maximum(s - 1, 0), 0)
    const2 = lambda s: (0, 0)
    const3 = lambda s: (0, 0, 0)
    per_layer = lambda s: (layer, 0, 0)
    state_specs = [
        pl.BlockSpec((HEADS, DQK, DV), const3),
        pl.BlockSpec((HEADS, 1, LANES), const3),
        pl.BlockSpec((HEADS, 1, LANES), const3),
        pl.BlockSpec((SUBLANES, CONV_W), const2),
    ]
    state_shapes = [
        jax.ShapeDtypeStruct((HEADS, DQK, DV), F32),
        jax.ShapeDtypeStruct((HEADS, 1, LANES), F32),
        jax.ShapeDtypeStruct((HEADS, 1, LANES), F32),
        jax.ShapeDtypeStruct((SUBLANES, CONV_W), F32),
    ]
    out = pl.pallas_call(
        functools.partial(_mixout_kernel, n_valid=n_valid, n_chunks=n_chunks,
                          n_steps=n_steps),
        grid=(n_steps + 1,),
        in_specs=[
            pl.BlockSpec((CHUNK, PROJ_W), mix_rows),
            pl.BlockSpec((CHUNK, LANES), mix_rows),
            pl.BlockSpec((None, 1, LANES), per_layer),
            pl.BlockSpec((None, SUBLANES, CONV_W), per_layer),
            pl.BlockSpec((None, 1, MLSTM_W), per_layer),
        ] + state_specs + [
            pl.BlockSpec((CHUNK, D_MODEL), out_rows),
            pl.BlockSpec((None, D_MODEL, D_MODEL), per_layer),
            pl.BlockSpec((None, 1, D_MODEL), per_layer),
        ],
        out_specs=[pl.BlockSpec((CHUNK, D_MODEL), out_rows),
                   pl.BlockSpec((CHUNK, D_MODEL), out_rows)] + state_specs,
        out_shape=[jax.ShapeDtypeStruct((m, D_MODEL), F32),
                   jax.ShapeDtypeStruct((m, D_MODEL), BF16)] + state_shapes,
        scratch_shapes=[
            pltpu.VMEM((HEADS, DQK, DV), F32),
            pltpu.VMEM((HEADS, 1, LANES), F32),
            pltpu.VMEM((HEADS, 1, LANES), F32),
            pltpu.VMEM((SUBLANES + CHUNK, CONV_W), F32),
            pltpu.VMEM((CHUNK, D_MODEL), BF16),
            pltpu.VMEM((CHUNK, D_MODEL), BF16),
        ],
        compiler_params=_params("arbitrary"),
        name="mixout",
    )(proj, gates, bias, convw, mnw, *state, h, wo, nwf)
    return out[0], out[1], tuple(out[2:])


def _ffn_kernel(*refs, tm, norm_out, cast):
    if cast == "next":
        (hf_ref, h_hbm, wg_ref, wu_ref, wd_ref, nw_ref, ng_ref, nu_ref, nd_ref,
         out_ref, cg_ref, cu_ref, cd_ref, hbuf, sem) = refs
    elif cast == "self":
        (hf_ref, h_hbm, wg_ref, wu_ref, wd_ref, nw_ref,
         out_ref, cg_ref, cu_ref, cd_ref, hbuf, sem) = refs
    else:
        hf_ref, h_hbm, wg_ref, wu_ref, wd_ref, nw_ref, out_ref, hbuf, sem = refs
    i = pl.program_id(0)
    k = pl.program_id(1)

    def residual_copy(tile):
        return pltpu.make_async_copy(h_hbm.at[pl.ds(tile * tm, tm), :], hbuf, sem)

    def contribution():
        wg, wu, wd = wg_ref[...], wu_ref[...], wd_ref[...]
        if cast == "next":
            cg_ref[...] = ng_ref[...].astype(BF16)
            cu_ref[...] = nu_ref[...].astype(BF16)
            cd_ref[...] = nd_ref[...].astype(BF16)
        elif cast == "self":
            wg, wu, wd = wg.astype(BF16), wu.astype(BF16), wd.astype(BF16)
            cg_ref[...] = wg
            cu_ref[...] = wu
            cd_ref[...] = wd
        x = hf_ref[...]
        g = jnp.dot(x, wg, preferred_element_type=F32)
        u = jnp.dot(x, wu, preferred_element_type=F32)
        act = (g * jax.nn.sigmoid(g) * u).astype(BF16)
        return jnp.dot(act, wd, preferred_element_type=F32)

    @pl.when(k == 0)
    def _():
        @pl.when(i == 0)
        def _():
            residual_copy(0).start()

        residual_copy(i).wait()
        out_ref[...] = hbuf[...] + contribution()

    @pl.when(k > 0)
    def _():
        out_ref[...] += contribution()

    @pl.when(jnp.logical_and(k == RESIDUAL_PREFETCH_STEP, i + 1 < pl.num_programs(0)))
    def _():
        residual_copy(i + 1).start()

    if norm_out:
        @pl.when(k == pl.num_programs(1) - 1)
        def _():
            out_ref[...] = _rms_scale(out_ref[...], nw_ref[...])


def _ffn(hf, h, weights, layer, nw_out, *, tm, tf, norm_out, cast=None, cast_from=None):
    wg, wu, wd = weights
    m = hf.shape[0]
    d_ff = wg.shape[2]
    n_i, n_k = m // tm, d_ff // tf
    assert RESIDUAL_PREFETCH_STEP < n_k
    rows = lambda i, k: (i, 0)
    in_specs = [
        pl.BlockSpec((tm, D_MODEL), rows),
        pl.BlockSpec(memory_space=pl.ANY),
        pl.BlockSpec((None, D_MODEL, tf), lambda i, k: (layer, 0, k)),
        pl.BlockSpec((None, D_MODEL, tf), lambda i, k: (layer, 0, k)),
        pl.BlockSpec((None, tf, D_MODEL), lambda i, k: (layer, k, 0)),
        pl.BlockSpec((1, D_MODEL), lambda i, k: (0, 0)),
    ]
    out_specs = [pl.BlockSpec((tm, D_MODEL), rows)]
    out_shape = [jax.ShapeDtypeStruct((m, D_MODEL), F32)]
    operands = [hf, h, wg, wu, wd, nw_out]
    cast_shapes = [
        jax.ShapeDtypeStruct((1, D_MODEL, d_ff), BF16),
        jax.ShapeDtypeStruct((1, D_MODEL, d_ff), BF16),
        jax.ShapeDtypeStruct((1, d_ff, D_MODEL), BF16),
    ]
    if cast == "next":
        ng, nu, nd, nxt = cast_from
        rb = D_MODEL // n_i
        assert rb * n_i == D_MODEL and rb % LANES == 0
        in_specs += [
            pl.BlockSpec((None, rb, tf), lambda i, k: (nxt, i, k)),
            pl.BlockSpec((None, rb, tf), lambda i, k: (nxt, i, k)),
            pl.BlockSpec((None, tf, rb), lambda i, k: (nxt, k, i)),
        ]
        out_specs += [
            pl.BlockSpec((None, rb, tf), lambda i, k: (0, i, k)),
            pl.BlockSpec((None, rb, tf), lambda i, k: (0, i, k)),
            pl.BlockSpec((None, tf, rb), lambda i, k: (0, k, i)),
        ]
        out_shape += cast_shapes
        operands += [ng, nu, nd]
    elif cast == "self":
        assert n_i == 1
        out_specs += [
            pl.BlockSpec((None, D_MODEL, tf), lambda i, k: (0, 0, k)),
            pl.BlockSpec((None, D_MODEL, tf), lambda i, k: (0, 0, k)),
            pl.BlockSpec((None, tf, D_MODEL), lambda i, k: (0, k, 0)),
        ]
        out_shape += cast_shapes
    out = pl.pallas_call(
        functools.partial(_ffn_kernel, tm=tm, norm_out=norm_out, cast=cast),
        grid=(n_i, n_k),
        in_specs=in_specs,
        out_specs=out_specs,
        out_shape=out_shape,
        scratch_shapes=[pltpu.VMEM((tm, D_MODEL), F32),
                        pltpu.SemaphoreType.DMA(())],
        compiler_params=_params("arbitrary", "arbitrary"),
        name="ffn",
    )(*operands)
    return out[0], tuple(out[1:])


def kernel(x, meta_tokens, norm_mix_w, w_in, b_gates, conv_w, mlstm_norm_w, w_out,
           norm_ffn_w, w_gate, w_up, w_down, norm_final_w):
    bsz, seq, _ = x.shape
    depth = w_in.shape[0]
    assert seq % CHUNK == 0 and N_META <= META_ROWS

    h_x = x.reshape(bsz * seq, D_MODEL)
    h_m = jnp.concatenate(
        [meta_tokens.astype(x.dtype), jnp.zeros((META_ROWS - N_META, D_MODEL), x.dtype)])

    w_main, w_gates = _pack_win(jnp.swapaxes(w_in, 1, 2), tb=512)
    wo = _cast_bf16(w_out, rb=512)
    ffn_w = None
    bias = jnp.pad(b_gates, ((0, 0), (0, LANES - N_GATES)))[:, None, :]
    convw = jnp.pad(conv_w, ((0, 0), (0, SUBLANES - CONV_K), (0, 0)))
    nw_mix = norm_mix_w[:, None, :]
    nw_ffn = norm_ffn_w[:, None, :]
    mnw = mlstm_norm_w[:, None, :]
    nw_final = norm_final_w[None, :]

    zero_state = (jnp.zeros((HEADS, DQK, DV), F32),
                  jnp.zeros((HEADS, 1, LANES), F32),
                  jnp.zeros((HEADS, 1, LANES), F32),
                  jnp.zeros((SUBLANES, CONV_W), F32))

    for l in range(depth):
        last = l == depth - 1
        proj, gates = _inproj(h_m, nw_mix, w_main, w_gates, l, tm=META_ROWS, tn=1024)
        h_m, hf, state = _mixout(proj, gates, bias, convw, mnw, zero_state,
                                 h_m, wo, nw_ffn, l, n_seq=1, n_valid=N_META)
        if ffn_w is None:
            h_m, ffn_w = _ffn(hf, h_m, (w_gate, w_up, w_down), l, nw_final,
                              tm=META_ROWS, tf=512, norm_out=False, cast="self")
        else:
            h_m, _ = _ffn(hf, h_m, ffn_w, 0, nw_final, tm=META_ROWS, tf=512,
                          norm_out=False)

        proj, gates = _inproj(h_x, nw_mix, w_main, w_gates, l, tm=1024, tn=2048)
        h_x, hf, _ = _mixout(proj, gates, bias, convw, mnw, state,
                             h_x, wo, nw_ffn, l, n_seq=bsz, n_valid=CHUNK)
        if last:
            h_x, _ = _ffn(hf, h_x, ffn_w, 0, nw_final, tm=1024, tf=512, norm_out=True)
        else:
            h_x, ffn_w = _ffn(hf, h_x, ffn_w, 0, nw_final, tm=1024, tf=512,
                              norm_out=False, cast="next",
                              cast_from=(w_gate, w_up, w_down, l + 1))

    return h_x.reshape(bsz, seq, D_MODEL)
```

```python
import functools
import math

import jax
import jax.numpy as jnp
from jax import lax
from jax.experimental import pallas as pl
from jax.experimental.pallas import tpu as pltpu

D_MODEL = 2048
N_META = 16
MLSTM_W = D_MODEL // 2
CONV_W = D_MODEL - MLSTM_W
HEADS = 4
DV = MLSTM_W // HEADS
DQK = DV // 2
QK_W = HEADS * DQK
CONV_K = 3
GATE_CAP = 15.0
EPS = 1e-6
QK_SCALE = DQK ** -0.5
LOG_QK_SCALE = math.log(QK_SCALE)

OFF_Q = 0
OFF_K = QK_W
OFF_V = 2 * QK_W
OFF_OG = OFF_V + MLSTM_W
OFF_U = OFF_OG + MLSTM_W
OFF_GB = OFF_U + CONV_W
OFF_GC = OFF_GB + CONV_W
PROJ_W = OFF_GC + CONV_W
GATE_COL0 = OFF_OG + MLSTM_W
N_GATES = 2 * HEADS
D_IN = PROJ_W + N_GATES

LANES = 128
SUBLANES = 8
CHUNK = 256
OUT_BLOCK = D_MODEL // HEADS
RESIDUAL_PREFETCH_STEP = 2
META_ROWS = CHUNK
VMEM_LIMIT = 56 * 1024 * 1024

F32 = jnp.float32
BF16 = jnp.bfloat16


def _params(*semantics):
    return pltpu.CompilerParams(dimension_semantics=semantics,
                                vmem_limit_bytes=VMEM_LIMIT)


def _rms_scale(x, w):
    ms = jnp.mean(x * x, axis=-1, keepdims=True)
    return x * lax.rsqrt(ms + EPS) * w


def _cast_kernel(x_ref, o_ref):
    o_ref[...] = x_ref[...].astype(BF16)


def _cast_bf16(w, *, rb, n_layers=None):
    depth, rows, cols = w.shape
    n_layers = depth if n_layers is None else n_layers
    spec = pl.BlockSpec((None, rb, cols), lambda l, i: (l, i, 0))
    return pl.pallas_call(
        _cast_kernel,
        grid=(n_layers, rows // rb),
        in_specs=[spec],
        out_specs=spec,
        out_shape=jax.ShapeDtypeStruct((n_layers, rows, cols), BF16),
        compiler_params=_params("arbitrary", "arbitrary"),
        name="cast_bf16",
    )(w)


def _pack_win_kernel(a_ref, b_ref, main_ref, gate_ref, *, tb):
    j = pl.program_id(1)
    first_shifted = GATE_COL0 // tb

    @pl.when(j < first_shifted)
    def _():
        main_ref[...] = a_ref[...].T.astype(BF16)

    @pl.when(j >= first_shifted)
    def _():
        x = jnp.concatenate([a_ref[N_GATES:, :], b_ref[...]], axis=0)
        main_ref[...] = x.T.astype(BF16)

    @pl.when(j == first_shifted)
    def _():
        w = a_ref[0:LANES, :].T
        hi = w.astype(BF16).astype(F32)
        lo = pltpu.roll(w - hi, N_GATES, axis=1)
        lane = lax.broadcasted_iota(jnp.int32, w.shape, 1)
        gate_ref[...] = jnp.where(lane < N_GATES, hi,
                                  jnp.where(lane < 2 * N_GATES, lo, 0.0)).astype(BF16)


def _pack_win(w_in_t, *, tb):
    depth = w_in_t.shape[0]
    assert GATE_COL0 % tb == 0 and PROJ_W % tb == 0 and N_GATES == SUBLANES
    return pl.pallas_call(
        functools.partial(_pack_win_kernel, tb=tb),
        grid=(depth, PROJ_W // tb),
        in_specs=[
            pl.BlockSpec((None, tb, D_MODEL), lambda l, j: (l, j, 0)),
            pl.BlockSpec((None, N_GATES, D_MODEL),
                         lambda l, j: (l, (j + 1) * (tb // N_GATES), 0)),
        ],
        out_specs=[pl.BlockSpec((None, D_MODEL, tb), lambda l, j: (l, 0, j)),
                   pl.BlockSpec((None, D_MODEL, LANES), lambda l, j: (l, 0, 0))],
        out_shape=[jax.ShapeDtypeStruct((depth, D_MODEL, PROJ_W), BF16),
                   jax.ShapeDtypeStruct((depth, D_MODEL, LANES), BF16)],
        compiler_params=_params("arbitrary", "arbitrary"),
        name="pack_win",
    )(w_in_t, w_in_t)


def _inproj_kernel(h_hbm, nw_ref, w_ref, wg_ref, proj_ref, gates_ref, hn_ref, hbuf, sem,
                   *, tm):
    i = pl.program_id(0)

    def h_copy(tile):
        return pltpu.make_async_copy(h_hbm.at[pl.ds(tile * tm, tm), :], hbuf, sem)

    @pl.when(pl.program_id(1) == 0)
    def _():
        @pl.when(i == 0)
        def _():
            h_copy(0).start()

        h_copy(i).wait()
        hn = _rms_scale(hbuf[...], nw_ref[...]).astype(BF16)
        hn_ref[...] = hn
        gates_ref[...] = jnp.dot(hn, wg_ref[...], preferred_element_type=F32)

        @pl.when(i + 1 < pl.num_programs(0))
        def _():
            h_copy(i + 1).start()

    proj_ref[...] = jnp.dot(hn_ref[...], w_ref[...],
                            preferred_element_type=F32).astype(BF16)


def _inproj(h, nw, w, wg, layer, *, tm, tn):
    m = h.shape[0]
    return pl.pallas_call(
        functools.partial(_inproj_kernel, tm=tm),
        grid=(m // tm, PROJ_W // tn),
        in_specs=[
            pl.BlockSpec(memory_space=pl.ANY),
            pl.BlockSpec((None, 1, D_MODEL), lambda i, j: (layer, 0, 0)),
            pl.BlockSpec((None, D_MODEL, tn), lambda i, j: (layer, 0, j)),
            pl.BlockSpec((None, D_MODEL, LANES), lambda i, j: (layer, 0, 0)),
        ],
        out_specs=[
            pl.BlockSpec((tm, tn), lambda i, j: (i, j)),
            pl.BlockSpec((tm, LANES), lambda i, j: (i, 0)),
        ],
        out_shape=[
            jax.ShapeDtypeStruct((m, PROJ_W), BF16),
            jax.ShapeDtypeStruct((m, LANES), F32),
        ],
        scratch_shapes=[pltpu.VMEM((tm, D_MODEL), BF16),
                        pltpu.VMEM((tm, D_MODEL), F32),
                        pltpu.SemaphoreType.DMA(())],
        compiler_params=_params("arbitrary", "arbitrary"),
        name="inproj",
    )(h, nw, w, wg)


def _log_sigmoid(x):
    return jnp.minimum(x, 0.0) - jnp.log1p(jnp.exp(-jnp.abs(x)))


def _cumsum_rows(x, tri):
    hi = x.astype(BF16)
    r1 = x - hi.astype(F32)
    mid = r1.astype(BF16)
    lo = (r1 - mid.astype(F32)).astype(BF16)
    dot = functools.partial(jnp.dot, preferred_element_type=F32)
    return dot(tri, hi) + dot(tri, mid) + dot(tri, lo)


def _mixout_kernel(proj_ref, gates_ref, bias_ref, convw_ref, mnw_ref,
                   c0_ref, n0_ref, m0_ref, t0_ref, h_ref, wo_ref, nwf_ref,
                   hnew_ref, hf_ref, cn_ref, nn_ref, mn_ref, tn_ref,
                   c_scr, n_scr, m_scr, a_scr, y_prev, y_cur,
                   *, n_valid, n_chunks, n_steps):
    c = CHUNK
    s = pl.program_id(0)
    chunk = jnp.minimum(s, n_steps - 1)

    @pl.when(s == 0)
    def _():
        y_prev[...] = jnp.zeros_like(y_prev)

    @pl.when(jnp.logical_and(lax.rem(chunk, n_chunks) == 0, s < n_steps))
    def _():
        c_scr[...] = c0_ref[...]
        n_scr[...] = n0_ref[...]
        m_scr[...] = m0_ref[...]
        a_scr[0:SUBLANES, :] = t0_ref[...]

    graw = gates_ref[...]
    g = graw + pltpu.roll(graw, LANES - N_GATES, axis=1) + bias_ref[...]
    capped = GATE_CAP * jnp.tanh(g / GATE_CAP)
    lane = lax.broadcasted_iota(jnp.int32, (c, LANES), 1)
    log_i = capped
    log_f = _log_sigmoid(capped)
    if n_valid < c:
        row_ok = lax.broadcasted_iota(jnp.int32, (c, LANES), 0) < n_valid
        log_i = jnp.where(row_ok, log_i, -jnp.inf)
        log_f = jnp.where(row_ok, log_f, 0.0)
    row = lax.broadcasted_iota(jnp.int32, (c, c), 0)
    col = lax.broadcasted_iota(jnp.int32, (c, c), 1)
    causal = row >= col
    tri = jnp.where(causal, 1.0, 0.0).astype(BF16)
    bcum = _cumsum_rows(jnp.where(lane >= HEADS, log_f, 0.0), tri)
    x_c = jnp.where(lane < HEADS, log_i, bcum)
    x_r = x_c.T

    def out_block(idx):
        n0 = idx * OUT_BLOCK
        return jnp.dot(y_prev[...], wo_ref[:, n0:n0 + OUT_BLOCK],
                       preferred_element_type=F32)

    out_parts = []
    for hd in range(HEADS):
        out_parts.append(out_block(hd))
        q = proj_ref[:, OFF_Q + hd * DQK:OFF_Q + (hd + 1) * DQK]
        k = proj_ref[:, OFF_K + hd * DQK:OFF_K + (hd + 1) * DQK]
        v = proj_ref[:, OFF_V + hd * DV:OFF_V + (hd + 1) * DV]
        og = proj_ref[:, OFF_OG + hd * DV:OFF_OG + (hd + 1) * DV].astype(F32)
        li_c = x_c[:, hd:hd + 1]
        b_c = x_c[:, HEADS + hd:HEADS + hd + 1]
        li_r = x_r[hd:hd + 1, :]
        b_r = x_r[HEADS + hd:HEADS + hd + 1, :]
        c_st = c_scr[hd]
        n_st = n_scr[hd]
        m_st = m_scr[hd][:, 0:1]

        dmat = jnp.where(causal, b_c - b_r + li_r, -jnp.inf)
        inter = b_c + m_st
        m_t = jnp.maximum(inter, jnp.max(dmat, axis=-1, keepdims=True))
        w_inter = jnp.exp(inter - m_t)
        sc = lax.dot_general(q, k, (((1,), (1,)), ((), ())),
                             preferred_element_type=F32)
        s_w = sc * jnp.exp(dmat - (m_t - LOG_QK_SCALE))
        qc = jnp.dot(q, c_st.astype(BF16), preferred_element_type=F32)
        num = (w_inter * QK_SCALE) * qc + jnp.dot(s_w.astype(BF16), v,
                                                   preferred_element_type=F32)
        qn = jnp.sum(q.astype(F32) * n_st, axis=-1, keepdims=True) * QK_SCALE
        den = w_inter * qn + jnp.sum(s_w, axis=-1, keepdims=True)
        hval = num / jnp.maximum(jnp.abs(den), jnp.exp(-m_t))

        b_end = b_c[c - 1:c, :]
        decay = b_end - b_c + li_c
        m_new = jnp.maximum(b_end + m_st, jnp.max(decay, axis=0, keepdims=True))
        w_old = jnp.exp(b_end + m_st - m_new)
        kw = k.astype(F32) * jnp.exp(decay - m_new)
        c_scr[hd] = w_old * c_st + jnp.dot(kw.T.astype(BF16), v,
                                           preferred_element_type=F32)
        n_scr[hd] = w_old * n_st + jnp.sum(kw, axis=0, keepdims=True)
        m_scr[hd] = jnp.broadcast_to(m_new, (1, LANES))

        hm = _rms_scale(hval, mnw_ref[:, hd * DV:(hd + 1) * DV])
        y_cur[:, hd * DV:(hd + 1) * DV] = (jax.nn.sigmoid(og) * hm).astype(BF16)

    u = proj_ref[:, OFF_U:OFF_U + CONV_W].astype(F32)
    gb = proj_ref[:, OFF_GB:OFF_GB + CONV_W].astype(F32)
    gc = proj_ref[:, OFF_GC:OFF_GC + CONV_W].astype(F32)
    a = gc * u
    a_scr[SUBLANES:SUBLANES + c, :] = a
    a1 = a_scr[SUBLANES - 1:SUBLANES - 1 + c, :]
    a2 = a_scr[SUBLANES - 2:SUBLANES - 2 + c, :]
    conv = a2 * convw_ref[0:1, :] + a1 * convw_ref[1:2, :] + a * convw_ref[2:3, :]
    y_cur[:, MLSTM_W:MLSTM_W + CONV_W] = (gb * conv).astype(BF16)
    a_scr[0:SUBLANES, :] = a_scr[n_valid:n_valid + SUBLANES, :]

    hnew = h_ref[...] + jnp.concatenate(out_parts, axis=1)
    hnew_ref[...] = hnew
    hf_ref[...] = _rms_scale(hnew, nwf_ref[...]).astype(BF16)

    y_prev[...] = y_cur[...]

    @pl.when(s == n_steps - 1)
    def _():
        cn_ref[...] = c_scr[...]
        nn_ref[...] = n_scr[...]
        mn_ref[...] = m_scr[...]
        tn_ref[...] = a_scr[0:SUBLANES, :]


def _mixout(proj, gates, bias, convw, mnw, state, h, wo, nwf, layer, *, n_seq, n_valid):
    m = proj.shape[0]
    n_steps = m // CHUNK
    n_chunks = n_steps // n_seq
    mix_rows = lambda s: (jnp.minimum(s, n_steps - 1), 0)
    out_rows = lambda s: (jnp.maximum(s - 1, 0), 0)
    const2 = lambda s: (0, 0)
    const3 = lambda s: (0, 0, 0)
    per_layer = lambda s: (layer, 0, 0)
    state_specs = [
        pl.BlockSpec((HEADS, DQK, DV), const3),
        pl.BlockSpec((HEADS, 1, LANES), const3),
        pl.BlockSpec((HEADS, 1, LANES), const3),
        pl.BlockSpec((SUBLANES, CONV_W), const2),
    ]
    state_shapes = [
        jax.ShapeDtypeStruct((HEADS, DQK, DV), F32),
        jax.ShapeDtypeStruct((HEADS, 1, LANES), F32),
        jax.ShapeDtypeStruct((HEADS, 1, LANES), F32),
        jax.ShapeDtypeStruct((SUBLANES, CONV_W), F32),
    ]
    out = pl.pallas_call(
        functools.partial(_mixout_kernel, n_valid=n_valid, n_chunks=n_chunks,
                          n_steps=n_steps),
        grid=(n_steps + 1,),
        in_specs=[
            pl.BlockSpec((CHUNK, PROJ_W), mix_rows),
            pl.BlockSpec((CHUNK, LANES), mix_rows),
            pl.BlockSpec((None, 1, LANES), per_layer),
            pl.BlockSpec((None, SUBLANES, CONV_W), per_layer),
            pl.BlockSpec((None, 1, MLSTM_W), per_layer),
        ] + state_specs + [
            pl.BlockSpec((CHUNK, D_MODEL), out_rows),
            pl.BlockSpec((None, D_MODEL, D_MODEL), per_layer),
            pl.BlockSpec((None, 1, D_MODEL), per_layer),
        ],
        out_specs=[pl.BlockSpec((CHUNK, D_MODEL), out_rows),
                   pl.BlockSpec((CHUNK, D_MODEL), out_rows)] + state_specs,
        out_shape=[jax.ShapeDtypeStruct((m, D_MODEL), F32),
                   jax.ShapeDtypeStruct((m, D_MODEL), BF16)] + state_shapes,
        scratch_shapes=[
            pltpu.VMEM((HEADS, DQK, DV), F32),
            pltpu.VMEM((HEADS, 1, LANES), F32),
            pltpu.VMEM((HEADS, 1, LANES), F32),
            pltpu.VMEM((SUBLANES + CHUNK, CONV_W), F32),
            pltpu.VMEM((CHUNK, D_MODEL), BF16),
            pltpu.VMEM((CHUNK, D_MODEL), BF16),
        ],
        compiler_params=_params("arbitrary"),
        name="mixout",
    )(proj, gates, bias, convw, mnw, *state, h, wo, nwf)
    return out[0], out[1], tuple(out[2:])


def _ffn_kernel(*refs, tm, norm_out, cast):
    if cast == "next":
        (hf_ref, h_hbm, wg_ref, wu_ref, wd_ref, nw_ref, ng_ref, nu_ref, nd_ref,
         out_ref, cg_ref, cu_ref, cd_ref, hbuf, sem) = refs
    elif cast == "self":
        (hf_ref, h_hbm, wg_ref, wu_ref, wd_ref, nw_ref,
         out_ref, cg_ref, cu_ref, cd_ref, hbuf, sem) = refs
    else:
        hf_ref, h_hbm, wg_ref, wu_ref, wd_ref, nw_ref, out_ref, hbuf, sem = refs
    i = pl.program_id(0)
    k = pl.program_id(1)

    def residual_copy(tile):
        return pltpu.make_async_copy(h_hbm.at[pl.ds(tile * tm, tm), :], hbuf, sem)

    def contribution():
        wg, wu, wd = wg_ref[...], wu_ref[...], wd_ref[...]
        if cast == "next":
            cg_ref[...] = ng_ref[...].astype(BF16)
            cu_ref[...] = nu_ref[...].astype(BF16)
            cd_ref[...] = nd_ref[...].astype(BF16)
        elif cast == "self":
            wg, wu, wd = wg.astype(BF16), wu.astype(BF16), wd.astype(BF16)
            cg_ref[...] = wg
            cu_ref[...] = wu
            cd_ref[...] = wd
        x = hf_ref[...]
        g = jnp.dot(x, wg, preferred_element_type=F32)
        u = jnp.dot(x, wu, preferred_element_type=F32)
        act = (g * jax.nn.sigmoid(g) * u).astype(BF16)
        return jnp.dot(act, wd, preferred_element_type=F32)

    @pl.when(k == 0)
    def _():
        @pl.when(i == 0)
        def _():
            residual_copy(0).start()

        residual_copy(i).wait()
        out_ref[...] = hbuf[...] + contribution()

    @pl.when(k > 0)
    def _():
        out_ref[...] += contribution()

    @pl.when(jnp.logical_and(k == RESIDUAL_PREFETCH_STEP, i + 1 < pl.num_programs(0)))
    def _():
        residual_copy(i + 1).start()

    if norm_out:
        @pl.when(k == pl.num_programs(1) - 1)
        def _():
            out_ref[...] = _rms_scale(out_ref[...], nw_ref[...])


def _ffn(hf, h, weights, layer, nw_out, *, tm, tf, norm_out, cast=None, cast_from=None):
    wg, wu, wd = weights
    m = hf.shape[0]
    d_ff = wg.shape[2]
    n_i, n_k = m // tm, d_ff // tf
    assert RESIDUAL_PREFETCH_STEP < n_k
    rows = lambda i, k: (i, 0)
    in_specs = [
        pl.BlockSpec((tm, D_MODEL), rows),
        pl.BlockSpec(memory_space=pl.ANY),
        pl.BlockSpec((None, D_MODEL, tf), lambda i, k: (layer, 0, k)),
        pl.BlockSpec((None, D_MODEL, tf), lambda i, k: (layer, 0, k)),
        pl.BlockSpec((None, tf, D_MODEL), lambda i, k: (layer, k, 0)),
        pl.BlockSpec((1, D_MODEL), lambda i, k: (0, 0)),
    ]
    out_specs = [pl.BlockSpec((tm, D_MODEL), rows)]
    out_shape = [jax.ShapeDtypeStruct((m, D_MODEL), F32)]
    operands = [hf, h, wg, wu, wd, nw_out]
    cast_shapes = [
        jax.ShapeDtypeStruct((1, D_MODEL, d_ff), BF16),
        jax.ShapeDtypeStruct((1, D_MODEL, d_ff), BF16),
        jax.ShapeDtypeStruct((1, d_ff, D_MODEL), BF16),
    ]
    if cast == "next":
        ng, nu, nd, nxt = cast_from
        rb = D_MODEL // n_i
        assert rb * n_i == D_MODEL and rb % LANES == 0
        in_specs += [
            pl.BlockSpec((None, rb, tf), lambda i, k: (nxt, i, k)),
            pl.BlockSpec((None, rb, tf), lambda i, k: (nxt, i, k)),
            pl.BlockSpec((None, tf, rb), lambda i, k: (nxt, k, i)),
        ]
        out_specs += [
            pl.BlockSpec((None, rb, tf), lambda i, k: (0, i, k)),
            pl.BlockSpec((None, rb, tf), lambda i, k: (0, i, k)),
            pl.BlockSpec((None, tf, rb), lambda i, k: (0, k, i)),
        ]
        out_shape += cast_shapes
        operands += [ng, nu, nd]
    elif cast == "self":
        assert n_i == 1
        out_specs += [
            pl.BlockSpec((None, D_MODEL, tf), lambda i, k: (0, 0, k)),
            pl.BlockSpec((None, D_MODEL, tf), lambda i, k: (0, 0, k)),
            pl.BlockSpec((None, tf, D_MODEL), lambda i, k: (0, k, 0)),
        ]
        out_shape += cast_shapes
    out = pl.pallas_call(
        functools.partial(_ffn_kernel, tm=tm, norm_out=norm_out, cast=cast),
        grid=(n_i, n_k),
        in_specs=in_specs,
        out_specs=out_specs,
        out_shape=out_shape,
        scratch_shapes=[pltpu.VMEM((tm, D_MODEL), F32),
                        pltpu.SemaphoreType.DMA(())],
        compiler_params=_params("arbitrary", "arbitrary"),
        name="ffn",
    )(*operands)
    return out[0], tuple(out[1:])


def kernel(x, meta_tokens, norm_mix_w, w_in, b_gates, conv_w, mlstm_norm_w, w_out,
           norm_ffn_w, w_gate, w_up, w_down, norm_final_w):
    bsz, seq, _ = x.shape
    depth = w_in.shape[0]
    assert seq % CHUNK == 0 and N_META <= META_ROWS

    h_x = x.reshape(bsz * seq, D_MODEL)
    h_m = jnp.concatenate(
        [meta_tokens.astype(x.dtype), jnp.zeros((META_ROWS - N_META, D_MODEL), x.dtype)])

    w_main, w_gates = _pack_win(jnp.swapaxes(w_in, 1, 2), tb=512)
    wo = _cast_bf16(w_out, rb=512)
    ffn_w = None
    bias = jnp.pad(b_gates, ((0, 0), (0, LANES - N_GATES)))[:, None, :]
    convw = jnp.pad(conv_w, ((0, 0), (0, SUBLANES - CONV_K), (0, 0)))
    nw_mix = norm_mix_w[:, None, :]
    nw_ffn = norm_ffn_w[:, None, :]
    mnw = mlstm_norm_w[:, None, :]
    nw_final = norm_final_w[None, :]

    zero_state = (jnp.zeros((HEADS, DQK, DV), F32),
                  jnp.zeros((HEADS, 1, LANES), F32),
                  jnp.zeros((HEADS, 1, LANES), F32),
                  jnp.zeros((SUBLANES, CONV_W), F32))

    for l in range(depth):
        last = l == depth - 1
        proj, gates = _inproj(h_m, nw_mix, w_main, w_gates, l, tm=META_ROWS, tn=3072)
        h_m, hf, state = _mixout(proj, gates, bias, convw, mnw, zero_state,
                                 h_m, wo, nw_ffn, l, n_seq=1, n_valid=N_META)
        if ffn_w is None:
            h_m, ffn_w = _ffn(hf, h_m, (w_gate, w_up, w_down), l, nw_final,
                              tm=META_ROWS, tf=512, norm_out=False, cast="self")
        else:
            h_m, _ = _ffn(hf, h_m, ffn_w, 0, nw_final, tm=META_ROWS, tf=512,
                          norm_out=False)

        proj, gates = _inproj(h_x, nw_mix, w_main, w_gates, l, tm=1024, tn=3072)
        h_x, hf, _ = _mixout(proj, gates, bias, convw, mnw, state,
                             h_x, wo, nw_ffn, l, n_seq=bsz, n_valid=CHUNK)
        if last:
            h_x, _ = _ffn(hf, h_x, ffn_w, 0, nw_final, tm=1024, tf=512, norm_out=True)
        else:
            h_x, ffn_w = _ffn(hf, h_x, ffn_w, 0, nw_final, tm=1024, tf=512,
                              norm_out=False, cast="next",
                              cast_from=(w_gate, w_up, w_down, l + 1))

    return h_x.reshape(bsz, seq, D_MODEL)
```

```python
import functools
import math

import jax
import jax.numpy as jnp
from jax import lax
from jax.experimental import pallas as pl
from jax.experimental.pallas import tpu as pltpu

D_MODEL = 2048
N_META = 16
MLSTM_W = D_MODEL // 2
CONV_W = D_MODEL - MLSTM_W
HEADS = 4
DV = MLSTM_W // HEADS
DQK = DV // 2
QK_W = HEADS * DQK
CONV_K = 3
GATE_CAP = 15.0
EPS = 1e-6
QK_SCALE = DQK ** -0.5
LOG_QK_SCALE = math.log(QK_SCALE)

OFF_Q = 0
OFF_K = QK_W
OFF_V = 2 * QK_W
OFF_OG = OFF_V + MLSTM_W
OFF_U = OFF_OG + MLSTM_W
OFF_GB = OFF_U + CONV_W
OFF_GC = OFF_GB + CONV_W
PROJ_W = OFF_GC + CONV_W
GATE_COL0 = OFF_OG + MLSTM_W
N_GATES = 2 * HEADS
D_IN = PROJ_W + N_GATES

LANES = 128
SUBLANES = 8
CHUNK = 256
OUT_BLOCK = D_MODEL // HEADS
PREFETCH_DMA_PRIORITY = 1
RESIDUAL_PREFETCH_STEP = 2
META_ROWS = CHUNK
VMEM_LIMIT = 56 * 1024 * 1024

F32 = jnp.float32
BF16 = jnp.bfloat16


def _params(*semantics):
    return pltpu.CompilerParams(dimension_semantics=semantics,
                                vmem_limit_bytes=VMEM_LIMIT)


def _rms_scale(x, w):
    ms = jnp.mean(x * x, axis=-1, keepdims=True)
    return x * lax.rsqrt(ms + EPS) * w


def _cast_kernel(x_ref, o_ref):
    o_ref[...] = x_ref[...].astype(BF16)


def _cast_bf16(w, *, rb, n_layers=None):
    depth, rows, cols = w.shape
    n_layers = depth if n_layers is None else n_layers
    spec = pl.BlockSpec((None, rb, cols), lambda l, i: (l, i, 0))
    return pl.pallas_call(
        _cast_kernel,
        grid=(n_layers, rows // rb),
        in_specs=[spec],
        out_specs=spec,
        out_shape=jax.ShapeDtypeStruct((n_layers, rows, cols), BF16),
        compiler_params=_params("arbitrary", "arbitrary"),
        name="cast_bf16",
    )(w)


def _pack_win_kernel(a_ref, b_ref, main_ref, gate_ref, *, tb):
    j = pl.program_id(1)
    first_shifted = GATE_COL0 // tb

    @pl.when(j < first_shifted)
    def _():
        main_ref[...] = a_ref[...].T.astype(BF16)

    @pl.when(j >= first_shifted)
    def _():
        x = jnp.concatenate([a_ref[N_GATES:, :], b_ref[...]], axis=0)
        main_ref[...] = x.T.astype(BF16)

    @pl.when(j == first_shifted)
    def _():
        w = a_ref[0:LANES, :].T
        hi = w.astype(BF16).astype(F32)
        lo = pltpu.roll(w - hi, N_GATES, axis=1)
        lane = lax.broadcasted_iota(jnp.int32, w.shape, 1)
        gate_ref[...] = jnp.where(lane < N_GATES, hi,
                                  jnp.where(lane < 2 * N_GATES, lo, 0.0)).astype(BF16)


def _pack_win(w_in_t, *, tb):
    depth = w_in_t.shape[0]
    assert GATE_COL0 % tb == 0 and PROJ_W % tb == 0 and N_GATES == SUBLANES
    return pl.pallas_call(
        functools.partial(_pack_win_kernel, tb=tb),
        grid=(depth, PROJ_W // tb),
        in_specs=[
            pl.BlockSpec((None, tb, D_MODEL), lambda l, j: (l, j, 0)),
            pl.BlockSpec((None, N_GATES, D_MODEL),
                         lambda l, j: (l, (j + 1) * (tb // N_GATES), 0)),
        ],
        out_specs=[pl.BlockSpec((None, D_MODEL, tb), lambda l, j: (l, 0, j)),
                   pl.BlockSpec((None, D_MODEL, LANES), lambda l, j: (l, 0, 0))],
        out_shape=[jax.ShapeDtypeStruct((depth, D_MODEL, PROJ_W), BF16),
                   jax.ShapeDtypeStruct((depth, D_MODEL, LANES), BF16)],
        compiler_params=_params("arbitrary", "arbitrary"),
        name="pack_win",
    )(w_in_t, w_in_t)


def _inproj_kernel(h_hbm, nw_ref, w_ref, wg_ref, proj_ref, gates_ref, hn_ref, hbuf, sem,
                   *, tm):
    i = pl.program_id(0)

    def h_copy(tile):
        return pltpu.make_async_copy(h_hbm.at[pl.ds(tile * tm, tm), :], hbuf, sem)

    @pl.when(pl.program_id(1) == 0)
    def _():
        @pl.when(i == 0)
        def _():
            h_copy(0).start(priority=PREFETCH_DMA_PRIORITY)

        h_copy(i).wait()
        hn = _rms_scale(hbuf[...], nw_ref[...]).astype(BF16)
        hn_ref[...] = hn
        gates_ref[...] = jnp.dot(hn, wg_ref[...], preferred_element_type=F32)

        @pl.when(i + 1 < pl.num_programs(0))
        def _():
            h_copy(i + 1).start(priority=PREFETCH_DMA_PRIORITY)

    proj_ref[...] = jnp.dot(hn_ref[...], w_ref[...],
                            preferred_element_type=F32).astype(BF16)


def _inproj(h, nw, w, wg, layer, *, tm, tn):
    m = h.shape[0]
    return pl.pallas_call(
        functools.partial(_inproj_kernel, tm=tm),
        grid=(m // tm, PROJ_W // tn),
        in_specs=[
            pl.BlockSpec(memory_space=pl.ANY),
            pl.BlockSpec((None, 1, D_MODEL), lambda i, j: (layer, 0, 0)),
            pl.BlockSpec((None, D_MODEL, tn), lambda i, j: (layer, 0, j)),
            pl.BlockSpec((None, D_MODEL, LANES), lambda i, j: (layer, 0, 0)),
        ],
        out_specs=[
            pl.BlockSpec((tm, tn), lambda i, j: (i, j)),
            pl.BlockSpec((tm, LANES), lambda i, j: (i, 0)),
        ],
        out_shape=[
            jax.ShapeDtypeStruct((m, PROJ_W), BF16),
            jax.ShapeDtypeStruct((m, LANES), F32),
        ],
        scratch_shapes=[pltpu.VMEM((tm, D_MODEL), BF16),
                        pltpu.VMEM((tm, D_MODEL), F32),
                        pltpu.SemaphoreType.DMA(())],
        compiler_params=_params("arbitrary", "arbitrary"),
        name="inproj",
    )(h, nw, w, wg)


def _log_sigmoid(x):
    return jnp.minimum(x, 0.0) - jnp.log1p(jnp.exp(-jnp.abs(x)))


def _cumsum_rows(x, tri):
    hi = x.astype(BF16)
    r1 = x - hi.astype(F32)
    mid = r1.astype(BF16)
    lo = (r1 - mid.astype(F32)).astype(BF16)
    dot = functools.partial(jnp.dot, preferred_element_type=F32)
    return dot(tri, hi) + dot(tri, mid) + dot(tri, lo)


def _mixout_kernel(proj_ref, gates_ref, bias_ref, convw_ref, mnw_ref,
                   c0_ref, n0_ref, m0_ref, t0_ref, h_ref, wo_ref, nwf_ref,
                   hnew_ref, hf_ref, cn_ref, nn_ref, mn_ref, tn_ref,
                   c_scr, n_scr, m_scr, a_scr, y_prev, y_cur,
                   *, n_valid, n_chunks, n_steps):
    c = CHUNK
    s = pl.program_id(0)
    chunk = jnp.minimum(s, n_steps - 1)

    @pl.when(s == 0)
    def _():
        y_prev[...] = jnp.zeros_like(y_prev)

    @pl.when(jnp.logical_and(lax.rem(chunk, n_chunks) == 0, s < n_steps))
    def _():
        c_scr[...] = c0_ref[...]
        n_scr[...] = n0_ref[...]
        m_scr[...] = m0_ref[...]
        a_scr[0:SUBLANES, :] = t0_ref[...]

    graw = gates_ref[...]
    g = graw + pltpu.roll(graw, LANES - N_GATES, axis=1) + bias_ref[...]
    capped = GATE_CAP * jnp.tanh(g / GATE_CAP)
    lane = lax.broadcasted_iota(jnp.int32, (c, LANES), 1)
    log_i = capped
    log_f = _log_sigmoid(capped)
    if n_valid < c:
        row_ok = lax.broadcasted_iota(jnp.int32, (c, LANES), 0) < n_valid
        log_i = jnp.where(row_ok, log_i, -jnp.inf)
        log_f = jnp.where(row_ok, log_f, 0.0)
    row = lax.broadcasted_iota(jnp.int32, (c, c), 0)
    col = lax.broadcasted_iota(jnp.int32, (c, c), 1)
    causal = row >= col
    tri = jnp.where(causal, 1.0, 0.0).astype(BF16)
    bcum = _cumsum_rows(jnp.where(lane >= HEADS, log_f, 0.0), tri)
    x_c = jnp.where(lane < HEADS, log_i, bcum)
    x_r = x_c.T

    def out_block(idx):
        n0 = idx * OUT_BLOCK
        return jnp.dot(y_prev[...], wo_ref[:, n0:n0 + OUT_BLOCK],
                       preferred_element_type=F32)

    out_parts = []
    for hd in range(HEADS):
        out_parts.append(out_block(hd))
        q = proj_ref[:, OFF_Q + hd * DQK:OFF_Q + (hd + 1) * DQK]
        k = proj_ref[:, OFF_K + hd * DQK:OFF_K + (hd + 1) * DQK]
        v = proj_ref[:, OFF_V + hd * DV:OFF_V + (hd + 1) * DV]
        og = proj_ref[:, OFF_OG + hd * DV:OFF_OG + (hd + 1) * DV].astype(F32)
        li_c = x_c[:, hd:hd + 1]
        b_c = x_c[:, HEADS + hd:HEADS + hd + 1]
        li_r = x_r[hd:hd + 1, :]
        b_r = x_r[HEADS + hd:HEADS + hd + 1, :]
        c_st = c_scr[hd]
        n_st = n_scr[hd]
        m_st = m_scr[hd][:, 0:1]

        dmat = jnp.where(causal, b_c - b_r + li_r, -jnp.inf)
        inter = b_c + m_st
        m_t = jnp.maximum(inter, jnp.max(dmat, axis=-1, keepdims=True))
        w_inter = jnp.exp(inter - m_t)
        sc = lax.dot_general(q, k, (((1,), (1,)), ((), ())),
                             preferred_element_type=F32)
        s_w = sc * jnp.exp(dmat - (m_t - LOG_QK_SCALE))
        qc = jnp.dot(q, c_st.astype(BF16), preferred_element_type=F32)
        num = (w_inter * QK_SCALE) * qc + jnp.dot(s_w.astype(BF16), v,
                                                   preferred_element_type=F32)
        qn = jnp.sum(q.astype(F32) * n_st, axis=-1, keepdims=True) * QK_SCALE
        den = w_inter * qn + jnp.sum(s_w, axis=-1, keepdims=True)
        hval = num / jnp.maximum(jnp.abs(den), jnp.exp(-m_t))

        b_end = b_c[c - 1:c, :]
        decay = b_end - b_c + li_c
        m_new = jnp.maximum(b_end + m_st, jnp.max(decay, axis=0, keepdims=True))
        w_old = jnp.exp(b_end + m_st - m_new)
        kw = k.astype(F32) * jnp.exp(decay - m_new)
        c_scr[hd] = w_old * c_st + jnp.dot(kw.T.astype(BF16), v,
                                           preferred_element_type=F32)
        n_scr[hd] = w_old * n_st + jnp.sum(kw, axis=0, keepdims=True)
        m_scr[hd] = jnp.broadcast_to(m_new, (1, LANES))

        hm = _rms_scale(hval, mnw_ref[:, hd * DV:(hd + 1) * DV])
        y_cur[:, hd * DV:(hd + 1) * DV] = (jax.nn.sigmoid(og) * hm).astype(BF16)

    u = proj_ref[:, OFF_U:OFF_U + CONV_W].astype(F32)
    gb = proj_ref[:, OFF_GB:OFF_GB + CONV_W].astype(F32)
    gc = proj_ref[:, OFF_GC:OFF_GC + CONV_W].astype(F32)
    a = gc * u
    a_scr[SUBLANES:SUBLANES + c, :] = a
    a1 = a_scr[SUBLANES - 1:SUBLANES - 1 + c, :]
    a2 = a_scr[SUBLANES - 2:SUBLANES - 2 + c, :]
    conv = a2 * convw_ref[0:1, :] + a1 * convw_ref[1:2, :] + a * convw_ref[2:3, :]
    y_cur[:, MLSTM_W:MLSTM_W + CONV_W] = (gb * conv).astype(BF16)
    a_scr[0:SUBLANES, :] = a_scr[n_valid:n_valid + SUBLANES, :]

    hnew = h_ref[...] + jnp.concatenate(out_parts, axis=1)
    hnew_ref[...] = hnew
    hf_ref[...] = _rms_scale(hnew, nwf_ref[...]).astype(BF16)

    y_prev[...] = y_cur[...]

    @pl.when(s == n_steps - 1)
    def _():
        cn_ref[...] = c_scr[...]
        nn_ref[...] = n_scr[...]
        mn_ref[...] = m_scr[...]
        tn_ref[...] = a_scr[0:SUBLANES, :]


def _mixout(proj, gates, bias, convw, mnw, state, h, wo, nwf, layer, *, n_seq, n_valid):
    m = proj.shape[0]
    n_steps = m // CHUNK
    n_chunks = n_steps // n_seq
    mix_rows = lambda s: (jnp.minimum(s, n_steps - 1), 0)
    out_rows = lambda s: (jnp.maximum(s - 1, 0), 0)
    const2 = lambda s: (0, 0)
    const3 = lambda s: (0, 0, 0)
    per_layer = lambda s: (layer, 0, 0)
    state_specs = [
        pl.BlockSpec((HEADS, DQK, DV), const3),
        pl.BlockSpec((HEADS, 1, LANES), const3),
        pl.BlockSpec((HEADS, 1, LANES), const3),
        pl.BlockSpec((SUBLANES, CONV_W), const2),
    ]
    state_shapes = [
        jax.ShapeDtypeStruct((HEADS, DQK, DV), F32),
        jax.ShapeDtypeStruct((HEADS, 1, LANES), F32),
        jax.ShapeDtypeStruct((HEADS, 1, LANES), F32),
        jax.ShapeDtypeStruct((SUBLANES, CONV_W), F32),
    ]
    out = pl.pallas_call(
        functools.partial(_mixout_kernel, n_valid=n_valid, n_chunks=n_chunks,
                          n_steps=n_steps),
        grid=(n_steps + 1,),
        in_specs=[
            pl.BlockSpec((CHUNK, PROJ_W), mix_rows),
            pl.BlockSpec((CHUNK, LANES), mix_rows),
            pl.BlockSpec((None, 1, LANES), per_layer),
            pl.BlockSpec((None, SUBLANES, CONV_W), per_layer),
            pl.BlockSpec((None, 1, MLSTM_W), per_layer),
        ] + state_specs + [
            pl.BlockSpec((CHUNK, D_MODEL), out_rows),
            pl.BlockSpec((None, D_MODEL, D_MODEL), per_layer),
            pl.BlockSpec((None, 1, D_MODEL), per_layer),
        ],
        out_specs=[pl.BlockSpec((CHUNK, D_MODEL), out_rows),
                   pl.BlockSpec((CHUNK, D_MODEL), out_rows)] + state_specs,
        out_shape=[jax.ShapeDtypeStruct((m, D_MODEL), F32),
                   jax.ShapeDtypeStruct((m, D_MODEL), BF16)] + state_shapes,
        scratch_shapes=[
            pltpu.VMEM((HEADS, DQK, DV), F32),
            pltpu.VMEM((HEADS, 1, LANES), F32),
            pltpu.VMEM((HEADS, 1, LANES), F32),
            pltpu.VMEM((SUBLANES + CHUNK, CONV_W), F32),
            pltpu.VMEM((CHUNK, D_MODEL), BF16),
            pltpu.VMEM((CHUNK, D_MODEL), BF16),
        ],
        compiler_params=_params("arbitrary"),
        name="mixout",
    )(proj, gates, bias, convw, mnw, *state, h, wo, nwf)
    return out[0], out[1], tuple(out[2:])


def _ffn_kernel(*refs, tm, norm_out, cast):
    if cast == "next":
        (hf_ref, h_hbm, wg_ref, wu_ref, wd_ref, nw_ref, ng_ref, nu_ref, nd_ref,
         out_ref, cg_ref, cu_ref, cd_ref, hbuf, sem) = refs
    elif cast == "self":
        (hf_ref, h_hbm, wg_ref, wu_ref, wd_ref, nw_ref,
         out_ref, cg_ref, cu_ref, cd_ref, hbuf, sem) = refs
    else:
        hf_ref, h_hbm, wg_ref, wu_ref, wd_ref, nw_ref, out_ref, hbuf, sem = refs
    i = pl.program_id(0)
    k = pl.program_id(1)

    def residual_copy(tile):
        return pltpu.make_async_copy(h_hbm.at[pl.ds(tile * tm, tm), :], hbuf, sem)

    def contribution():
        wg, wu, wd = wg_ref[...], wu_ref[...], wd_ref[...]
        if cast == "next":
            cg_ref[...] = ng_ref[...].astype(BF16)
            cu_ref[...] = nu_ref[...].astype(BF16)
            cd_ref[...] = nd_ref[...].astype(BF16)
        elif cast == "self":
            wg, wu, wd = wg.astype(BF16), wu.astype(BF16), wd.astype(BF16)
            cg_ref[...] = wg
            cu_ref[...] = wu
            cd_ref[...] = wd
        x = hf_ref[...]
        g = jnp.dot(x, wg, preferred_element_type=F32)
        u = jnp.dot(x, wu, preferred_element_type=F32)
        act = (g * jax.nn.sigmoid(g) * u).astype(BF16)
        return jnp.dot(act, wd, preferred_element_type=F32)

    @pl.when(k == 0)
    def _():
        @pl.when(i == 0)
        def _():
            residual_copy(0).start(priority=PREFETCH_DMA_PRIORITY)

        residual_copy(i).wait()
        out_ref[...] = hbuf[...] + contribution()

    @pl.when(k > 0)
    def _():
        out_ref[...] += contribution()

    @pl.when(jnp.logical_and(k == RESIDUAL_PREFETCH_STEP, i + 1 < pl.num_programs(0)))
    def _():
        residual_copy(i + 1).start(priority=PREFETCH_DMA_PRIORITY)

    if norm_out:
        @pl.when(k == pl.num_programs(1) - 1)
        def _():
            out_ref[...] = _rms_scale(out_ref[...], nw_ref[...])


def _ffn(hf, h, weights, layer, nw_out, *, tm, tf, norm_out, cast=None, cast_from=None):
    wg, wu, wd = weights
    m = hf.shape[0]
    d_ff = wg.shape[2]
    n_i, n_k = m // tm, d_ff // tf
    assert RESIDUAL_PREFETCH_STEP < n_k
    rows = lambda i, k: (i, 0)
    in_specs = [
        pl.BlockSpec((tm, D_MODEL), rows),
        pl.BlockSpec(memory_space=pl.ANY),
        pl.BlockSpec((None, D_MODEL, tf), lambda i, k: (layer, 0, k)),
        pl.BlockSpec((None, D_MODEL, tf), lambda i, k: (layer, 0, k)),
        pl.BlockSpec((None, tf, D_MODEL), lambda i, k: (layer, k, 0)),
        pl.BlockSpec((1, D_MODEL), lambda i, k: (0, 0)),
    ]
    out_specs = [pl.BlockSpec((tm, D_MODEL), rows)]
    out_shape = [jax.ShapeDtypeStruct((m, D_MODEL), F32)]
    operands = [hf, h, wg, wu, wd, nw_out]
    cast_shapes = [
        jax.ShapeDtypeStruct((1, D_MODEL, d_ff), BF16),
        jax.ShapeDtypeStruct((1, D_MODEL, d_ff), BF16),
        jax.ShapeDtypeStruct((1, d_ff, D_MODEL), BF16),
    ]
    if cast == "next":
        ng, nu, nd, nxt = cast_from
        rb = D_MODEL // n_i
        assert rb * n_i == D_MODEL and rb % LANES == 0
        in_specs += [
            pl.BlockSpec((None, rb, tf), lambda i, k: (nxt, i, k)),
            pl.BlockSpec((None, rb, tf), lambda i, k: (nxt, i, k)),
            pl.BlockSpec((None, tf, rb), lambda i, k: (nxt, k, i)),
        ]
        out_specs += [
            pl.BlockSpec((None, rb, tf), lambda i, k: (0, i, k)),
            pl.BlockSpec((None, rb, tf), lambda i, k: (0, i, k)),
            pl.BlockSpec((None, tf, rb), lambda i, k: (0, k, i)),
        ]
        out_shape += cast_shapes
        operands += [ng, nu, nd]
    elif cast == "self":
        assert n_i == 1
        out_specs += [
            pl.BlockSpec((None, D_MODEL, tf), lambda i, k: (0, 0, k)),
            pl.BlockSpec((None, D_MODEL, tf), lambda i, k: (0, 0, k)),
            pl.BlockSpec((None, tf, D_MODEL), lambda i, k: (0, k, 0)),
        ]
        out_shape += cast_shapes
    out = pl.pallas_call(
        functools.partial(_ffn_kernel, tm=tm, norm_out=norm_out, cast=cast),
        grid=(n_i, n_k),
        in_specs=in_specs,
        out_specs=out_specs,
        out_shape=out_shape,
        scratch_shapes=[pltpu.VMEM((tm, D_MODEL), F32),
                        pltpu.SemaphoreType.DMA(())],
        compiler_params=_params("arbitrary", "arbitrary"),
        name="ffn",
    )(*operands)
    return out[0], tuple(out[1:])


def kernel(x, meta_tokens, norm_mix_w, w_in, b_gates, conv_w, mlstm_norm_w, w_out,
           norm_ffn_w, w_gate, w_up, w_down, norm_final_w):
    bsz, seq, _ = x.shape
    depth = w_in.shape[0]
    assert seq % CHUNK == 0 and N_META <= META_ROWS

    h_x = x.reshape(bsz * seq, D_MODEL)
    h_m = jnp.concatenate(
        [meta_tokens.astype(x.dtype), jnp.zeros((META_ROWS - N_META, D_MODEL), x.dtype)])

    w_main, w_gates = _pack_win(jnp.swapaxes(w_in, 1, 2), tb=512)
    wo = _cast_bf16(w_out, rb=512)
    ffn_w = None
    bias = jnp.pad(b_gates, ((0, 0), (0, LANES - N_GATES)))[:, None, :]
    convw = jnp.pad(conv_w, ((0, 0), (0, SUBLANES - CONV_K), (0, 0)))
    nw_mix = norm_mix_w[:, None, :]
    nw_ffn = norm_ffn_w[:, None, :]
    mnw = mlstm_norm_w[:, None, :]
    nw_final = norm_final_w[None, :]

    zero_state = (jnp.zeros((HEADS, DQK, DV), F32),
                  jnp.zeros((HEADS, 1, LANES), F32),
                  jnp.zeros((HEADS, 1, LANES), F32),
                  jnp.zeros((SUBLANES, CONV_W), F32))

    for l in range(depth):
        last = l == depth - 1
        proj, gates = _inproj(h_m, nw_mix, w_main, w_gates, l, tm=META_ROWS, tn=3072)
        h_m, hf, state = _mixout(proj, gates, bias, convw, mnw, zero_state,
                                 h_m, wo, nw_ffn, l, n_seq=1, n_valid=N_META)
        if ffn_w is None:
            h_m, ffn_w = _ffn(hf, h_m, (w_gate, w_up, w_down), l, nw_final,
                              tm=META_ROWS, tf=512, norm_out=False, cast="self")
        else:
            h_m, _ = _ffn(hf, h_m, ffn_w, 0, nw_final, tm=META_ROWS, tf=512,
                          norm_out=False)

        proj, gates = _inproj(h_x, nw_mix, w_main, w_gates, l, tm=1024, tn=3072)
        h_x, hf, _ = _mixout(proj, gates, bias, convw, mnw, state,
                             h_x, wo, nw_ffn, l, n_seq=bsz, n_valid=CHUNK)
        if last:
            h_x, _ = _ffn(hf, h_x, ffn_w, 0, nw_final, tm=1024, tf=512, norm_out=True)
        else:
            h_x, ffn_w = _ffn(hf, h_x, ffn_w, 0, nw_final, tm=1024, tf=512,
                              norm_out=False, cast="next",
                              cast_from=(w_gate, w_up, w_down, l + 1))

    return h_x.reshape(bsz, seq, D_MODEL)
```
